```python
import math
import jax, jax.numpy as jnp
from jax import lax
import numpy as np

D_MODEL = 1024
BATCH = 8
SEQ = 2048
DEPTH = 1
DEC_BATCH = 32
DEC_SEQ = 1
PAST_LEN = 16384
PAGE_SIZE = 128

N_HEADS = 8
HEAD_DIM = 64
ATTN_WIDTH = N_HEADS * HEAD_DIM
CONV_CH = D_MODEL // 2
CONV_WIDTH = 31
MOBA_BLOCK = 256
MOBA_TOPK = 3
Q_CHUNK = 16
D_FF = ((8 * D_MODEL // 3 + 255) // 256) * 256
EPS = 1e-6
NEG = -1e30

kernel_name = 'moba_conformer_hybrid_step'


def rms_norm(x, g):
    xf = x.astype(jnp.float32)
    y = xf * lax.rsqrt(jnp.mean(xf * xf, axis=-1, keepdims=True) + EPS)
    return (y * g.astype(jnp.float32)).astype(x.dtype)


def layer_norm(x, g, b):
    xf = x.astype(jnp.float32)
    mu = jnp.mean(xf, axis=-1, keepdims=True)
    var = jnp.mean(jnp.square(xf - mu), axis=-1, keepdims=True)
    y = (xf - mu) * lax.rsqrt(var + EPS)
    return (y * g.astype(jnp.float32) + b.astype(jnp.float32)).astype(x.dtype)


def alibi_slopes():
    return jnp.exp2(-8.0 * jnp.arange(1, N_HEADS + 1, dtype=jnp.float32) / N_HEADS)


def moba_attention(q, k, v, q_offset, slopes):
    B, Sq, H, Dh = q.shape
    T = k.shape[1]
    nb = -(-T // MOBA_BLOCK)
    pad = nb * MOBA_BLOCK - T
    kp = jnp.pad(k, ((0, 0), (0, pad), (0, 0), (0, 0)))
    vp = jnp.pad(v, ((0, 0), (0, pad), (0, 0), (0, 0)))
    kb = kp.reshape(B, nb, MOBA_BLOCK, H, Dh)
    vb = vp.reshape(B, nb, MOBA_BLOCK, H, Dh)
    k_mean = jnp.mean(kb.astype(jnp.float32), axis=2)
    t_q = q_offset + jnp.arange(Sq, dtype=jnp.int32)
    q_blk = t_q // MOBA_BLOCK
    gate = jnp.einsum('bshd,bnhd->bhsn', q.astype(jnp.float32), k_mean)
    fully_past = jnp.arange(nb, dtype=jnp.int32)[None, :] < q_blk[:, None]
    gate = jnp.where(fully_past[None, None], gate, NEG)
    ks = min(MOBA_TOPK, nb)
    _, idx = lax.top_k(gate, ks)
    qc = math.gcd(Q_CHUNK, Sq)
    nc = Sq // qc
    bi = jnp.arange(B)[:, None, None, None]
    hi = jnp.arange(H)[None, :, None, None]
    scale = Dh ** -0.5
    j_pos = jnp.arange(MOBA_BLOCK, dtype=jnp.int32)

    def chunk(ci):
        s0 = ci * qc
        q_c = lax.dynamic_slice_in_dim(q, s0, qc, axis=1).astype(jnp.float32) * scale
        idx_c = lax.dynamic_slice_in_dim(idx, s0, qc, axis=2)
        t_c = q_offset + s0 + jnp.arange(qc, dtype=jnp.int32)
        blk_c = t_c // MOBA_BLOCK
        k_sel = kb[bi, idx_c, :, hi].astype(jnp.float32)
        v_sel = vb[bi, idx_c, :, hi].astype(jnp.float32)
        s_sel = jnp.einsum('bqhd,bhqnjd->bhqnj', q_c, k_sel)
        pos_sel = idx_c[..., None] * MOBA_BLOCK + j_pos
        ok_sel = idx_c[..., None] < blk_c[None, None, :, None, None]
        dist_sel = (t_c[None, None, :, None, None] - pos_sel).astype(jnp.float32)
        s_sel = jnp.where(ok_sel, s_sel - slopes[None, :, None, None, None] * dist_sel, NEG)
        own0 = ((q_offset + s0) // MOBA_BLOCK) * MOBA_BLOCK
        k_own = lax.dynamic_slice_in_dim(kp, own0, MOBA_BLOCK, axis=1).astype(jnp.float32)
        v_own = lax.dynamic_slice_in_dim(vp, own0, MOBA_BLOCK, axis=1).astype(jnp.float32)
        s_own = jnp.einsum('bqhd,bjhd->bhqj', q_c, k_own)
        dist_own = (t_c[:, None] - (own0 + j_pos)[None, :]).astype(jnp.float32)
        s_own = jnp.where(dist_own[None, None] >= 0.0,
                          s_own - slopes[None, :, None, None] * dist_own[None, None], NEG)
        scores = jnp.concatenate([s_sel.reshape(B, H, qc, ks * MOBA_BLOCK), s_own], axis=-1)
        p = jax.nn.softmax(scores, axis=-1)
        p_sel = p[..., :ks * MOBA_BLOCK].reshape(B, H, qc, ks, MOBA_BLOCK)
        p_own = p[..., ks * MOBA_BLOCK:]
        o = (jnp.einsum('bhqnj,bhqnjd->bqhd', p_sel, v_sel)
             + jnp.einsum('bhqj,bjhd->bqhd', p_own, v_own))
        return o.astype(q.dtype)

    out = lax.map(chunk, jnp.arange(nc, dtype=jnp.int32))
    return out.transpose(1, 0, 2, 3, 4).reshape(B, Sq, H * Dh)


def depthwise_causal_conv(buf, w):
    return lax.conv_general_dilated(buf, w, window_strides=(1,), padding='VALID',
                                    dimension_numbers=('NWC', 'WIO', 'NWC'),
                                    feature_group_count=CONV_CH)


def hybrid_layer(x, c, k_past, v_past, conv_past, pos0, w_ada, b_ada, g_norm1, w_in, g_q, g_k,
                 w_attn_proj, w_dwconv, b_dwconv, g_conv_ln, b_conv_ln, w_conv_proj, w_out,
                 g_norm2, w_ffn_in, w_ffn_out):
    B, S, _ = x.shape
    mod = jax.nn.silu(c) @ w_ada + b_ada
    sh1, sc1, gt1, sh2, sc2, gt2 = [m[:, None, :] for m in jnp.split(mod, 6, axis=-1)]
    h = rms_norm(x, g_norm1) * (1.0 + sc1) + sh1
    proj = h @ w_in
    A, C, D = ATTN_WIDTH, CONV_CH, D_MODEL
    q, k, v, glu_a, glu_b, gate_a, gate_c = jnp.split(
        proj, [A, 2 * A, 3 * A, 3 * A + C, 3 * A + 2 * C, 3 * A + 2 * C + D], axis=-1)
    q = rms_norm(q.reshape(B, S, N_HEADS, HEAD_DIM), g_q)
    k = rms_norm(k.reshape(B, S, N_HEADS, HEAD_DIM), g_k)
    v = v.reshape(B, S, N_HEADS, HEAD_DIM)
    k_all = jnp.concatenate([k_past, k], axis=1)
    v_all = jnp.concatenate([v_past, v], axis=1)
    attn = moba_attention(q, k_all, v_all, pos0, alibi_slopes()) @ w_attn_proj
    u = glu_a * jax.nn.sigmoid(glu_b)
    conv_in = jnp.concatenate([conv_past, u], axis=1)
    cv = depthwise_causal_conv(conv_in, w_dwconv) + b_dwconv
    cv = jax.nn.silu(layer_norm(cv, g_conv_ln, b_conv_ln)) @ w_conv_proj
    mix = (jax.nn.sigmoid(gate_a) * attn + jax.nn.sigmoid(gate_c) * cv) @ w_out
    x = x + gt1 * mix
    h2 = rms_norm(x, g_norm2) * (1.0 + sc2) + sh2
    f_g, f_u = jnp.split(h2 @ w_ffn_in, 2, axis=-1)
    x = x + gt2 * ((jax.nn.silu(f_g) * f_u) @ w_ffn_out)
    return x, k, v, conv_in[:, -(CONV_WIDTH - 1):]


def setup_inputs(seed: int = 0) -> dict:
    key = jax.random.key(seed)
    ks = jax.random.split(key, 24)
    f32 = jnp.float32
    n_pages = PAST_LEN // PAGE_SIZE
    n_used = DEC_BATCH * n_pages
    n_phys = n_used + max(1, n_used // 4)
    nrm = lambda k, shape, s: jax.random.normal(k, shape, f32) * s
    page_table = jax.random.permutation(ks[0], n_phys)[:n_used].reshape(DEC_BATCH, n_pages).astype(jnp.int32)
    w_in_cols = 3 * ATTN_WIDTH + 2 * CONV_CH + 2 * D_MODEL
    return {
        'x_prompt': nrm(ks[1], (BATCH, SEQ, D_MODEL), 1.0),
        'x_sample': nrm(ks[2], (DEC_BATCH, DEC_SEQ, D_MODEL), 1.0),
        'cache_k': nrm(ks[3], (n_phys, PAGE_SIZE, N_HEADS, HEAD_DIM), 1.0),
        'cache_v': nrm(ks[4], (n_phys, PAGE_SIZE, N_HEADS, HEAD_DIM), 1.0),
        'state_conv': nrm(ks[5], (DEC_BATCH, CONV_WIDTH - 1, CONV_CH), 0.5),
        'page_table': page_table,
        'c_prompt': nrm(ks[6], (BATCH, D_MODEL), 1.0),
        'c_sample': nrm(ks[7], (DEC_BATCH, D_MODEL), 1.0),
        'w_ada': nrm(ks[8], (D_MODEL, 6 * D_MODEL), 0.5 * D_MODEL ** -0.5),
        'b_ada': nrm(ks[9], (6 * D_MODEL,), 0.01),
        'g_norm1': 1.0 + nrm(ks[10], (D_MODEL,), 0.02),
        'w_in': nrm(ks[11], (D_MODEL, w_in_cols), D_MODEL ** -0.5),
        'g_q': 1.0 + nrm(ks[12], (HEAD_DIM,), 0.02),
        'g_k': 1.0 + nrm(ks[13], (HEAD_DIM,), 0.02),
        'w_attn_proj': nrm(ks[14], (ATTN_WIDTH, D_MODEL), ATTN_WIDTH ** -0.5),
        'w_dwconv': nrm(ks[15], (CONV_WIDTH, 1, CONV_CH), CONV_WIDTH ** -0.5),
        'b_dwconv': nrm(ks[16], (CONV_CH,), 0.01),
        'g_conv_ln': 1.0 + nrm(ks[17], (CONV_CH,), 0.02),
        'b_conv_ln': nrm(ks[18], (CONV_CH,), 0.01),
        'w_conv_proj': nrm(ks[19], (CONV_CH, D_MODEL), CONV_CH ** -0.5),
        'w_out': nrm(ks[20], (D_MODEL, D_MODEL), D_MODEL ** -0.5),
        'g_norm2': 1.0 + nrm(ks[21], (D_MODEL,), 0.02),
        'w_ffn_in': nrm(ks[22], (D_MODEL, 2 * D_FF), D_MODEL ** -0.5),
        'w_ffn_out': nrm(ks[23], (D_FF, D_MODEL), D_FF ** -0.5),
    }


def reference(x_prompt, x_sample, cache_k, cache_v, state_conv, page_table, c_prompt, c_sample,
              w_ada, b_ada, g_norm1, w_in, g_q, g_k, w_attn_proj, w_dwconv, b_dwconv,
              g_conv_ln, b_conv_ln, w_conv_proj, w_out, g_norm2, w_ffn_in, w_ffn_out):
    weights = (w_ada, b_ada, g_norm1, w_in, g_q, g_k, w_attn_proj, w_dwconv, b_dwconv,
               g_conv_ln, b_conv_ln, w_conv_proj, w_out, g_norm2, w_ffn_in, w_ffn_out)
    B = x_prompt.shape[0]
    DB = x_sample.shape[0]
    k_past0 = jnp.zeros((B, 0, N_HEADS, HEAD_DIM), x_prompt.dtype)
    v_past0 = jnp.zeros((B, 0, N_HEADS, HEAD_DIM), x_prompt.dtype)
    conv0 = jnp.zeros((B, CONV_WIDTH - 1, CONV_CH), x_prompt.dtype)
    y_prompt = x_prompt
    for _ in range(DEPTH):
        y_prompt, k_prompt, v_prompt, conv_prompt = hybrid_layer(
            y_prompt, c_prompt, k_past0, v_past0, conv0, 0, *weights)
    k_past = cache_k[page_table].reshape(DB, -1, N_HEADS, HEAD_DIM)
    v_past = cache_v[page_table].reshape(DB, -1, N_HEADS, HEAD_DIM)
    y_sample = x_sample
    for _ in range(DEPTH):
        y_sample, k_sample, v_sample, conv_sample = hybrid_layer(
            y_sample, c_sample, k_past, v_past, state_conv, PAST_LEN, *weights)
    return (y_prompt, y_sample, k_prompt, v_prompt, conv_prompt, k_sample, v_sample, conv_sample)
```

```python
import functools

import jax
import jax.numpy as jnp
from jax import lax
from jax.experimental import pallas as pl
from jax.experimental.pallas import tpu as pltpu

D_MODEL = 1024
N_HEADS = 8
HEAD_DIM = 64
ATTN_WIDTH = N_HEADS * HEAD_DIM
CONV_CH = 512
CONV_WIDTH = 31
CONV_HALO = 32
MOBA_BLOCK = 256
MOBA_TOPK = 3
PAGE_SIZE = 128
PAGES_PER_BLOCK = MOBA_BLOCK // PAGE_SIZE
D_FF = 2816
FFN_CHUNK = 256
EPS = 1e-6
NEG = -1e30
SM_SCALE = HEAD_DIM ** -0.5

LANES = 128
HEADS_PER_LANE_TILE = LANES // HEAD_DIM
VMEM_LIMIT = 56 * 1024 * 1024

F32 = jnp.float32
BF16 = jnp.bfloat16
HIGHEST = lax.Precision.HIGHEST


def _sigmoid(x):
    return 1.0 / (1.0 + jnp.exp(-x))


def _silu(x):
    return x * _sigmoid(x)


def _dot(a, b):
    return jnp.dot(a, b, preferred_element_type=F32)


def _dot_nt(a, b, precision=None):
    return lax.dot_general(a, b, (((1,), (1,)), ((), ())), precision=precision,
                           preferred_element_type=F32)


def _iota(shape, dim):
    return lax.broadcasted_iota(jnp.int32, shape, dim)


def _params(semantics):
    return pltpu.CompilerParams(dimension_semantics=semantics, vmem_limit_bytes=VMEM_LIMIT)


def _ada_kernel(cp_ref, cs_ref, w_ref, b_ref, mp_ref, ms_ref):
    w = w_ref[...]
    b = b_ref[...]
    mp_ref[...] = jnp.dot(_silu(cp_ref[...]), w, precision=HIGHEST, preferred_element_type=F32) + b
    ms_ref[...] = jnp.dot(_silu(cs_ref[...]), w, precision=HIGHEST, preferred_element_type=F32) + b


def _ada(c_prompt, c_sample, w_ada, b_ada):
    nb, ns = c_prompt.shape[0], c_sample.shape[0]
    n_out = w_ada.shape[1]
    tn = D_MODEL
    return pl.pallas_call(
        _ada_kernel,
        grid=(n_out // tn,),
        in_specs=[
            pl.BlockSpec((nb, D_MODEL), lambda j: (0, 0)),
            pl.BlockSpec((ns, D_MODEL), lambda j: (0, 0)),
            pl.BlockSpec((D_MODEL, tn), lambda j: (0, j)),
            pl.BlockSpec((1, tn), lambda j: (0, j)),
        ],
        out_specs=[
            pl.BlockSpec((nb, tn), lambda j: (0, j)),
            pl.BlockSpec((ns, tn), lambda j: (0, j)),
        ],
        out_shape=[jax.ShapeDtypeStruct((nb, n_out), F32), jax.ShapeDtypeStruct((ns, n_out), F32)],
        compiler_params=_params(("arbitrary",)),
        name="ada_mod",
    )(c_prompt, c_sample, w_ada, b_ada.reshape(1, n_out))


def _head_rms_norm(t, gain, group_ones):
    sq = t * t
    hi = sq.astype(BF16)
    lo = (sq - hi.astype(F32)).astype(BF16)
    ss = _dot(hi, group_ones) + _dot(lo, group_ones)
    return t * lax.rsqrt(ss * (1.0 / HEAD_DIM) + EPS) * gain


def _inproj_kernel(x_ref, sh_ref, sc_ref, g1_ref, w_ref, gq_ref, gk_ref, *out_refs, prompt):
    if prompt:
        q_ref, k_ref, v_ref, kb_ref, vb_ref, u_ref, sga_ref, sgc_ref, km_ref = out_refs
    else:
        q_ref, k_ref, v_ref, u_ref, sga_ref, sgc_ref = out_refs
    x = x_ref[0]
    ms = jnp.mean(x * x, axis=-1, keepdims=True)
    h = x * lax.rsqrt(ms + EPS) * g1_ref[...]
    h = h * (1.0 + sc_ref[0]) + sh_ref[0]
    hb = h.astype(BF16)

    def proj(c0, c1):
        return _dot(hb, w_ref[:, c0:c1])

    a, c, d = ATTN_WIDTH, CONV_CH, D_MODEL
    row = _iota((a, a), 0)
    col = _iota((a, a), 1)
    group_ones = jnp.where((row >> 6) == (col >> 6), 1.0, 0.0).astype(BF16)

    q = _head_rms_norm(proj(0, a), gq_ref[...], group_ones)
    k = _head_rms_norm(proj(a, 2 * a), gk_ref[...], group_ones)
    v = proj(2 * a, 3 * a)
    q_ref[0] = q
    k_ref[0] = k
    v_ref[0] = v
    glu_a = proj(3 * a, 3 * a + c)
    glu_b = proj(3 * a + c, 3 * a + 2 * c)
    u_ref[0] = glu_a * _sigmoid(glu_b)
    sga_ref[0] = _sigmoid(proj(3 * a + 2 * c, 3 * a + 2 * c + d))
    sgc_ref[0] = _sigmoid(proj(3 * a + 2 * c + d, 3 * a + 2 * c + 2 * d))
    if prompt:
        kb_ref[0] = k.astype(BF16)
        vb_ref[0] = v.astype(BF16)
        rows = k.shape[0]
        for r in range(rows // MOBA_BLOCK):
            km_ref[0, r] = jnp.mean(k[r * MOBA_BLOCK:(r + 1) * MOBA_BLOCK], axis=0, keepdims=True)


def _inproj(x3, mod3, g_norm1, w_in_b, gq_t, gk_t, *, tm, prompt):
    g, s, _ = x3.shape
    r = mod3.shape[1]
    nt = s // tm
    rows_blk = 1 if r == 1 else tm

    def mod_spec(chunk):
        if r == 1:
            return pl.BlockSpec((1, 1, D_MODEL), lambda b, t: (b, 0, chunk))
        return pl.BlockSpec((1, tm, D_MODEL), lambda b, t: (b, t, chunk))

    del rows_blk
    tok = lambda width: pl.BlockSpec((1, tm, width), lambda b, t: (b, t, 0))
    full = lambda shape: pl.BlockSpec(shape, lambda b, t: (0,) * len(shape))
    out_specs = [tok(ATTN_WIDTH), tok(ATTN_WIDTH), tok(ATTN_WIDTH)]
    out_shape = [jax.ShapeDtypeStruct((g, s, ATTN_WIDTH), F32)] * 3
    if prompt:
        out_specs += [tok(ATTN_WIDTH), tok(ATTN_WIDTH)]
        out_shape += [jax.ShapeDtypeStruct((g, s, ATTN_WIDTH), BF16)] * 2
    out_specs += [tok(CONV_CH), tok(D_MODEL), tok(D_MODEL)]
    out_shape += [jax.ShapeDtypeStruct((g, s, CONV_CH), F32),
                  jax.ShapeDtypeStruct((g, s, D_MODEL), F32),
                  jax.ShapeDtypeStruct((g, s, D_MODEL), F32)]
    if prompt:
        nblk = tm // MOBA_BLOCK
        out_specs += [pl.BlockSpec((1, nblk, 1, ATTN_WIDTH), lambda b, t: (b, t, 0, 0))]
        out_shape += [jax.ShapeDtypeStruct((g, s // MOBA_BLOCK, 1, ATTN_WIDTH), F32)]
    return pl.pallas_call(
        functools.partial(_inproj_kernel, prompt=prompt),
        grid=(g, nt),
        in_specs=[
            tok(D_MODEL),
            mod_spec(0),
            mod_spec(1),
            full((1, D_MODEL)),
            full(w_in_b.shape),
            full((1, ATTN_WIDTH)),
            full((1, ATTN_WIDTH)),
        ],
        out_specs=out_specs,
        out_shape=out_shape,
        compiler_params=_params(("arbitrary", "arbitrary")),
        name="in_proj_prompt" if prompt else "in_proj_sample",
    )(x3, mod3, mod3, g_norm1.reshape(1, D_MODEL), w_in_b, gq_t, gk_t)


def _moba_select(q, km, q_blk):
    nb = km.shape[0]
    nb_shift = nb.bit_length() - 1
    assert nb == 1 << nb_shift and N_HEADS * nb <= LANES
    kmt = jnp.concatenate([km] * N_HEADS + [jnp.zeros((LANES - N_HEADS * nb, ATTN_WIDTH), F32)], axis=0)
    row = _iota(kmt.shape, 0)
    col = _iota(kmt.shape, 1)
    kmt = jnp.where(((row >> nb_shift) == (col >> 6)) & (row < N_HEADS * nb), kmt, 0.0)
    gate = _dot_nt(q, kmt, precision=HIGHEST)
    lane = _iota(gate.shape, 1)
    n_idx = lane & (nb - 1)
    beaten_by = jnp.zeros(gate.shape, jnp.int32)
    for s in range(1, nb):
        lower = n_idx >= s
        g_m = jnp.where(lower, pltpu.roll(gate, s, axis=1), pltpu.roll(gate, LANES - nb + s, axis=1))
        m_idx = jnp.where(lower, n_idx - s, n_idx - s + nb)
        beats = (g_m > gate) | (lower & (g_m == gate))
        beaten_by += jnp.where(beats & (m_idx < q_blk), 1, 0)
    keep = (n_idx < q_blk) & (beaten_by < MOBA_TOPK) & (lane < N_HEADS * nb)
    return jnp.where(keep, 1.0, 0.0)


def _attn_kernel(q_ref, kb_ref, vb_ref, km_ref, o_ref):
    i = pl.program_id(1)
    q = q_ref[0]
    nb = km_ref.shape[1]
    keep = _moba_select(q, km_ref[0], i)
    keep_lane = _iota(keep.shape, 1)
    blk = MOBA_BLOCK
    rel = (_iota((blk, blk), 0) - _iota((blk, blk), 1)).astype(F32)
    lane = _iota((blk, LANES), 1)
    own_start = pl.multiple_of(i * blk, blk)

    for hp in range(N_HEADS // HEADS_PER_LANE_TILE):
        lanes = slice(hp * LANES, (hp + 1) * LANES)
        q_pair = q[:, lanes]
        outs = []
        for half in range(HEADS_PER_LANE_TILE):
            h = hp * HEADS_PER_LANE_TILE + half
            slope = 2.0 ** -(h + 1)
            qz = (jnp.where((lane >> 6) == half, q_pair, 0.0) * SM_SCALE).astype(BF16)

            s = _dot_nt(qz, kb_ref[0, pl.ds(own_start, blk), lanes])
            s = jnp.where(rel >= 0.0, s - slope * rel, NEG)
            m = jnp.max(s, axis=1, keepdims=True)
            p = jnp.exp(s - m)
            l = jnp.sum(p, axis=1, keepdims=True)
            acc = _dot(p.astype(BF16), vb_ref[0, pl.ds(own_start, blk), lanes])

            def past_block(jb, carry, h=h, slope=slope, qz=qz, lanes=lanes):
                m, l, acc = carry
                start = pl.multiple_of(jb * blk, blk)
                s = _dot_nt(qz, kb_ref[0, pl.ds(start, blk), lanes])
                dist = rel + ((i - jb) * blk).astype(F32)
                s = s - slope * dist
                kept = jnp.sum(jnp.where(keep_lane == h * nb + jb, keep, 0.0), axis=1, keepdims=True)
                s = jnp.where(kept > 0.5, s, NEG)
                m_new = jnp.maximum(m, jnp.max(s, axis=1, keepdims=True))
                alpha = jnp.exp(m - m_new)
                p = jnp.exp(s - m_new)
                l = alpha * l + jnp.sum(p, axis=1, keepdims=True)
                acc = alpha * acc + _dot(p.astype(BF16), vb_ref[0, pl.ds(start, blk), lanes])
                return m_new, l, acc

            m, l, acc = lax.fori_loop(0, i, past_block, (m, l, acc))
            outs.append(acc / l)
        o_ref[0, :, lanes] = jnp.where(lane < HEAD_DIM, outs[0], outs[1]).astype(BF16)


def _attn(q, kb, vb, km):
    b, s, _ = q.shape
    nq = s // MOBA_BLOCK
    return pl.pallas_call(
        _attn_kernel,
        grid=(b, nq),
        in_specs=[
            pl.BlockSpec((1, MOBA_BLOCK, ATTN_WIDTH), lambda bi, i: (bi, i, 0)),
            pl.BlockSpec((1, s, ATTN_WIDTH), lambda bi, i: (bi, 0, 0)),
            pl.BlockSpec((1, s, ATTN_WIDTH), lambda bi, i: (bi, 0, 0)),
            pl.BlockSpec((1, nq, ATTN_WIDTH), lambda bi, i: (bi, 0, 0)),
        ],
        out_specs=pl.BlockSpec((1, MOBA_BLOCK, ATTN_WIDTH), lambda bi, i: (bi, i, 0)),
        out_shape=jax.ShapeDtypeStruct((b, s, ATTN_WIDTH), BF16),
        compiler_params=_params(("arbitrary", "arbitrary")),
        name="moba_attn_prompt",
    )(q, kb, vb, km)


def _mix_tail(cv, attn_b, sga, sgc, x, gt1, gln, bln, wap_ref, wcp_ref, wo_ref):
    mu = jnp.mean(cv, axis=-1, keepdims=True)
    cen = cv - mu
    var = jnp.mean(cen * cen, axis=-1, keepdims=True)
    y = cen * lax.rsqrt(var + EPS) * gln + bln
    cvp = _dot(_silu(y).astype(BF16), wcp_ref[...])
    ap = _dot(attn_b, wap_ref[...])
    mix = _dot((sga * ap + sgc * cvp).astype(BF16), wo_ref[...])
    return x + gt1 * mix


def _mix_prompt_kernel(x_ref, attn_ref, u_ref, halo_ref, sga_ref, sgc_ref, gt1_ref, wdw_ref, bdw_ref,
                       gln_ref, bln_ref, wap_ref, wcp_ref, wo_ref, x1_ref, ext_ref):
    t = pl.program_id(1)
    tm = u_ref.shape[1]
    ext_ref[0:CONV_HALO, :] = jnp.where(t > 0, halo_ref[0], 0.0)
    ext_ref[CONV_HALO:, :] = u_ref[0]
    lead = CONV_HALO - (CONV_WIDTH - 1)
    cv = jnp.zeros((tm, CONV_CH), F32)
    for w in range(CONV_WIDTH):
        cv = cv + ext_ref[lead + w:lead + w + tm, :] * wdw_ref[w:w + 1, :]
    cv = cv + bdw_ref[...]
    x1_ref[0] = _mix_tail(cv, attn_ref[0], sga_ref[0], sgc_ref[0], x_ref[0], gt1_ref[0],
                          gln_ref[...], bln_ref[...], wap_ref, wcp_ref, wo_ref)


def _mix_prompt(x3, attn, u, sga, sgc, mod3, wdw, bdw, gln, bln, wap_b, wcp_b, wo_b, *, tm):
    b, s, _ = x3.shape
    nt = s // tm
    halo_per_tile = tm // CONV_HALO
    tok = lambda width: pl.BlockSpec((1, tm, width), lambda bi, t: (bi, t, 0))
    full = lambda shape: pl.BlockSpec(shape, lambda bi, t: (0,) * len(shape))
    return pl.pallas_call(
        _mix_prompt_kernel,
        grid=(b, nt),
        in_specs=[
            tok(D_MODEL), tok(ATTN_WIDTH), tok(CONV_CH),
            pl.BlockSpec((1, CONV_HALO, CONV_CH), lambda bi, t: (bi, jnp.maximum(t * halo_per_tile - 1, 0), 0)),
            tok(D_MODEL), tok(D_MODEL),
            pl.BlockSpec((1, 1, D_MODEL), lambda bi, t: (bi, 0, 2)),
            full(wdw.shape), full((1, CONV_CH)), full((1, CONV_CH)), full((1, CONV_CH)),
            full(wap_b.shape), full(wcp_b.shape), full(wo_b.shape),
        ],
        out_specs=tok(D_MODEL),
        out_shape=jax.ShapeDtypeStruct((b, s, D_MODEL), F32),
        scratch_shapes=[pltpu.VMEM((CONV_HALO + tm, CONV_CH), F32)],
        compiler_params=_params(("arbitrary", "arbitrary")),
        name="mix_prompt",
    )(x3, attn, u, u, sga, sgc, mod3, wdw, bdw, gln, bln, wap_b, wcp_b, wo_b)


def _mix_sample_kernel(x_ref, attn_ref, u_ref, st_ref, sga_ref, sgc_ref, gt1_ref, wdw_ref, bdw_ref,
                       gln_ref, bln_ref, wap_ref, wcp_ref, wo_ref, x1_ref, st_out_ref):
    u = u_ref[...]
    past = CONV_WIDTH - 1
    cv = u * wdw_ref[past:past + 1, :] + bdw_ref[...]
    for w in range(past):
        cv = cv + st_ref[w] * wdw_ref[w:w + 1, :]
    for w in range(1, past):
        st_out_ref[w - 1] = st_ref[w]
    st_out_ref[past - 1] = u
    x1_ref[...] = _mix_tail(cv, attn_ref[...].astype(BF16), sga_ref[...], sgc_ref[...], x_ref[...], gt1_ref[...],
                            gln_ref[...], bln_ref[...], wap_ref, wcp_ref, wo_ref)


def _mix_sample(x2, attn, u, st_t, sga, sgc, mod_s, wdw, bdw, gln, bln, wap_b, wcp_b, wo_b):
    n = x2.shape[0]
    full = lambda shape: pl.BlockSpec(shape, lambda j: (0,) * len(shape))
    return pl.pallas_call(
        _mix_sample_kernel,
        grid=(1,),
        in_specs=[
            full(x2.shape), full(attn.shape), full(u.shape), full(st_t.shape), full(sga.shape), full(sgc.shape),
            pl.BlockSpec((n, D_MODEL), lambda j: (0, 2)),
            full(wdw.shape), full((1, CONV_CH)), full((1, CONV_CH)), full((1, CONV_CH)),
            full(wap_b.shape), full(wcp_b.shape), full(wo_b.shape),
        ],
        out_specs=[full(x2.shape), full(st_t.shape)],
        out_shape=[jax.ShapeDtypeStruct(x2.shape, F32), jax.ShapeDtypeStruct(st_t.shape, F32)],
        compiler_params=_params(("arbitrary",)),
        name="mix_sample",
    )(x2, attn, u, st_t, sga, sgc, mod_s, wdw, bdw, gln, bln, wap_b, wcp_b, wo_b)


def _ffn_kernel(x_ref, sh_ref, sc_ref, gt_ref, g2_ref, wi_ref, wo_ref, y_ref):
    x = x_ref[0]
    ms = jnp.mean(x * x, axis=-1, keepdims=True)
    h = x * lax.rsqrt(ms + EPS) * g2_ref[...]
    hb = (h * (1.0 + sc_ref[0]) + sh_ref[0]).astype(BF16)
    acc = jnp.zeros(x.shape, F32)
    for c in range(D_FF // FFN_CHUNK):
        c0 = c * FFN_CHUNK
        f_g = _dot(hb, wi_ref[:, c0:c0 + FFN_CHUNK])
        f_u = _dot(hb, wi_ref[:, D_FF + c0:D_FF + c0 + FFN_CHUNK])
        acc = acc + _dot((_silu(f_g) * f_u).astype(BF16), wo_ref[c0:c0 + FFN_CHUNK, :])
    y_ref[0] = x + gt_ref[0] * acc


def _ffn(x3, mod3, g_norm2, wi_b, wo_b, *, tm, name):
    g, s, _ = x3.shape
    r = mod3.shape[1]
    nt = s // tm

    def mod_spec(chunk):
        if r == 1:
            return pl.BlockSpec((1, 1, D_MODEL), lambda b, t: (b, 0, chunk))
        return pl.BlockSpec((1, tm, D_MODEL), lambda b, t: (b, t, chunk))

    tok = pl.BlockSpec((1, tm, D_MODEL), lambda b, t: (b, t, 0))
    full = lambda shape: pl.BlockSpec(shape, lambda b, t: (0,) * len(shape))
    return pl.pallas_call(
        _ffn_kernel,
        grid=(g, nt),
        in_specs=[tok, mod_spec(3), mod_spec(4), mod_spec(5), full((1, D_MODEL)), full(wi_b.shape), full(wo_b.shape)],
        out_specs=tok,
        out_shape=jax.ShapeDtypeStruct(x3.shape, F32),
        compiler_params=_params(("arbitrary", "arbitrary")),
        name=name,
    )(x3, mod3, mod3, mod3, g_norm2.reshape(1, D_MODEL), wi_b, wo_b)


GATE_PAGES_PER_STEP = 16


def _gate_kernel(pt_ref, q_ref, *refs):
    del pt_ref
    k_refs = refs[:GATE_PAGES_PER_STEP]
    idx_ref = refs[GATE_PAGES_PER_STEP]
    qcol_ref, gate_ref = refs[GATE_PAGES_PER_STEP + 1:]
    step = pl.program_id(1)
    blocks_per_step = GATE_PAGES_PER_STEP // PAGES_PER_BLOCK

    @pl.when(step == 0)
    def _():
        qcol_ref[...] = jnp.broadcast_to(q_ref[0], (LANES, ATTN_WIDTH)).T
        gate_ref[...] = jnp.zeros(gate_ref.shape, F32)

    qcol = qcol_ref[...]
    sub = _iota((N_HEADS, LANES), 0)
    lane = _iota((N_HEADS, LANES), 1)
    gate = gate_ref[...]
    for blk in range(blocks_per_step):
        score = jnp.zeros((N_HEADS, LANES), F32)
        for pg in range(PAGES_PER_BLOCK):
            kt = k_refs[blk * PAGES_PER_BLOCK + pg][0].reshape(ATTN_WIDTH, PAGE_SIZE)
            prod = kt * qcol
            for h in range(N_HEADS):
                s_h = jnp.sum(prod[h * HEAD_DIM:(h + 1) * HEAD_DIM], axis=0, keepdims=True)
                score = score + jnp.where(sub == h, s_h, 0.0)
        g_blk = jnp.sum(score, axis=1, keepdims=True) * (1.0 / MOBA_BLOCK)
        gate = jnp.where(lane == step * blocks_per_step + blk, g_blk, gate)
    gate_ref[...] = gate

    @pl.when(step == pl.num_programs(1) - 1)
    def _():
        n_blocks = pl.num_programs(1) * blocks_per_step
        g = jnp.where(lane < n_blocks, gate, -jnp.inf)
        out = jnp.zeros((N_HEADS, LANES), jnp.int32)
        for r in range(MOBA_TOPK):
            best = jnp.max(g, axis=1, keepdims=True)
            pick = jnp.min(jnp.where(g == best, lane, LANES), axis=1, keepdims=True)
            out = jnp.where(lane == r, pick, out)
            g = jnp.where(lane == pick, -jnp.inf, g)
        idx_ref[0] = out


def _gate_topk(page_table, q_s3, cache_kt):
    nseq, n_pages = page_table.shape
    steps = n_pages // GATE_PAGES_PER_STEP

    def page_spec(r):
        return pl.BlockSpec((1, N_HEADS, HEAD_DIM, PAGE_SIZE),
                            lambda b, s, pt: (pt[b, s * GATE_PAGES_PER_STEP + r], 0, 0, 0))

    grid_spec = pltpu.PrefetchScalarGridSpec(
        num_scalar_prefetch=1,
        grid=(nseq, steps),
        in_specs=[pl.BlockSpec((1, 1, ATTN_WIDTH), lambda b, s, pt: (b, 0, 0))]
        + [page_spec(r) for r in range(GATE_PAGES_PER_STEP)],
        out_specs=pl.BlockSpec((1, N_HEADS, LANES), lambda b, s, pt: (b, 0, 0)),
        scratch_shapes=[pltpu.VMEM((ATTN_WIDTH, LANES), F32), pltpu.VMEM((N_HEADS, LANES), F32)],
    )
    return pl.pallas_call(
        _gate_kernel,
        grid_spec=grid_spec,
        out_shape=jax.ShapeDtypeStruct((nseq, N_HEADS, LANES), jnp.int32),
        compiler_params=_params(("arbitrary", "arbitrary")),
        name="moba_gate_sample",
    )(page_table, q_s3, *([cache_kt] * GATE_PAGES_PER_STEP))


SEL_PAGES = MOBA_TOPK * PAGES_PER_BLOCK


def _sattn_kernel(pt_ref, idx_ref, q_ref, kn_ref, vn_ref, *refs, past_len):
    del pt_ref
    k_refs = refs[:SEL_PAGES]
    v_refs = refs[SEL_PAGES:2 * SEL_PAGES]
    o_ref = refs[2 * SEL_PAGES]
    b = pl.program_id(0)
    h = pl.program_id(1)
    q = q_ref[0, 0] * SM_SCALE
    q8 = jnp.broadcast_to(q, (8, HEAD_DIM)).astype(BF16)
    lane = _iota((8, PAGE_SIZE), 1)
    slope = lax.bitcast_convert_type(jnp.full((8, PAGE_SIZE), 126, jnp.int32) - h << 23, F32)
    scores = []
    for r in range(MOBA_TOPK):
        blk = idx_ref[(b * N_HEADS + h) * MOBA_TOPK + r]
        for pg in range(PAGES_PER_BLOCK):
            s = _dot(q8, k_refs[r * PAGES_PER_BLOCK + pg][0, 0].astype(BF16))
            pos = blk * MOBA_BLOCK + pg * PAGE_SIZE + lane
            scores.append(s - slope * (past_len - pos).astype(F32))
    s_own = jnp.sum(q * kn_ref[0, 0], axis=1, keepdims=True)
    m = s_own
    for s in scores:
        m = jnp.maximum(m, jnp.max(s[0:1], axis=1, keepdims=True))
    p_own = jnp.exp(s_own - m)
    l = p_own
    acc = p_own * vn_ref[0, 0]
    for j, s in enumerate(scores):
        p = jnp.exp(s - m)
        l = l + jnp.sum(p[0:1], axis=1, keepdims=True)
        acc = acc + _dot_nt(p.astype(BF16), v_refs[j][0, 0].astype(BF16))[0:1]
    o_ref[0, 0] = acc / l


def _sattn(page_table, idx_flat, q4, kn4, vn4, cache_kt, cache_vt):
    nseq, n_pages = page_table.shape
    past_len = n_pages * PAGE_SIZE

    def page_spec(r, pg):
        return pl.BlockSpec(
            (1, 1, HEAD_DIM, PAGE_SIZE),
            lambda b, h, pt, idx: (pt[b, idx[(b * N_HEADS + h) * MOBA_TOPK + r] * PAGES_PER_BLOCK + pg], h, 0, 0))

    tok = pl.BlockSpec((1, 1, 1, HEAD_DIM), lambda b, h, pt, idx: (b, h, 0, 0))
    pages = [page_spec(r, pg) for r in range(MOBA_TOPK) for pg in range(PAGES_PER_BLOCK)]
    grid_spec = pltpu.PrefetchScalarGridSpec(
        num_scalar_prefetch=2,
        grid=(nseq, N_HEADS),
        in_specs=[tok, tok, tok] + pages + pages,
        out_specs=tok,
    )
    return pl.pallas_call(
        functools.partial(_sattn_kernel, past_len=past_len),
        grid_spec=grid_spec,
        out_shape=jax.ShapeDtypeStruct((nseq, N_HEADS, 1, HEAD_DIM), F32),
        compiler_params=_params(("arbitrary", "arbitrary")),
        name="moba_attn_sample",
    )(page_table, idx_flat, q4, kn4, vn4, *([cache_kt] * SEL_PAGES), *([cache_vt] * SEL_PAGES))


def kernel(x_prompt, x_sample, cache_k, cache_v, state_conv, page_table, c_prompt, c_sample, w_ada, b_ada, g_norm1, w_in, g_q, g_k, w_attn_proj, w_dwconv, b_dwconv, g_conv_ln, b_conv_ln, w_conv_proj, w_out, g_norm2, w_ffn_in, w_ffn_out):
    nb, seq, _ = x_prompt.shape
    ns = x_sample.shape[0]

    w_in_b = w_in.astype(BF16)
    wap_b = w_attn_proj.astype(BF16)
    wcp_b = w_conv_proj.astype(BF16)
    wo_b = w_out.astype(BF16)
    wfi_b = w_ffn_in.astype(BF16)
    wfo_b = w_ffn_out.astype(BF16)
    gq_t = jnp.tile(g_q, N_HEADS).reshape(1, ATTN_WIDTH)
    gk_t = jnp.tile(g_k, N_HEADS).reshape(1, ATTN_WIDTH)
    wdw = w_dwconv.reshape(CONV_WIDTH, CONV_CH)
    bdw = b_dwconv.reshape(1, CONV_CH)
    gln = g_conv_ln.reshape(1, CONV_CH)
    bln = b_conv_ln.reshape(1, CONV_CH)

    mod_p, mod_s = _ada(c_prompt, c_sample, w_ada, b_ada)
    mod_p3 = mod_p.reshape(nb, 1, 6 * D_MODEL)
    mod_s3 = mod_s.reshape(1, ns, 6 * D_MODEL)

    q, k, v, kb, vb, u, sga, sgc, km = _inproj(x_prompt, mod_p3, g_norm1, w_in_b, gq_t, gk_t, tm=256, prompt=True)
    attn = _attn(q, kb, vb, km.reshape(nb, seq // MOBA_BLOCK, ATTN_WIDTH))
    x1 = _mix_prompt(x_prompt, attn, u, sga, sgc, mod_p3, wdw, bdw, gln, bln, wap_b, wcp_b, wo_b, tm=256)
    y_prompt = _ffn(x1, mod_p3, g_norm2, wfi_b, wfo_b, tm=256, name="ffn_prompt")
    k_prompt = k.reshape(nb, seq, N_HEADS, HEAD_DIM)
    v_prompt = v.reshape(nb, seq, N_HEADS, HEAD_DIM)
    conv_prompt = u[:, seq - (CONV_WIDTH - 1):, :]

    xs3 = x_sample.reshape(1, ns, D_MODEL)
    qs, ks, vs, us, sga_s, sgc_s = _inproj(xs3, mod_s3, g_norm1, w_in_b, gq_t, gk_t, tm=ns, prompt=False)
    cache_kt = jnp.transpose(cache_k, (0, 2, 3, 1))
    cache_vt = jnp.transpose(cache_v, (0, 2, 3, 1))
    idx = _gate_topk(page_table, qs.reshape(ns, 1, ATTN_WIDTH), cache_kt)
    idx_flat = idx[:, :, :MOBA_TOPK].reshape(-1)
    head4 = lambda t: t.reshape(ns, N_HEADS, 1, HEAD_DIM)
    attn_s = _sattn(page_table, idx_flat, head4(qs), head4(ks), head4(vs), cache_kt, cache_vt)
    x1_s, st_new = _mix_sample(
        xs3[0], attn_s.reshape(ns, ATTN_WIDTH), us[0], jnp.transpose(state_conv, (1, 0, 2)), sga_s[0], sgc_s[0],
        mod_s, wdw, bdw, gln, bln, wap_b, wcp_b, wo_b)
    y_sample = _ffn(x1_s.reshape(1, ns, D_MODEL), mod_s3, g_norm2, wfi_b, wfo_b, tm=ns, name="ffn_sample")

    return (y_prompt, y_sample.reshape(ns, 1, D_MODEL), k_prompt, v_prompt, conv_prompt,
            ks.reshape(ns, 1, N_HEADS, HEAD_DIM), vs.reshape(ns, 1, N_HEADS, HEAD_DIM),
            jnp.transpose(st_new, (1, 0, 2)))
```

```python
import functools

import jax
import jax.numpy as jnp
from jax import lax
from jax.experimental import pallas as pl
from jax.experimental.pallas import tpu as pltpu

D_MODEL = 1024
N_HEADS = 8
HEAD_DIM = 64
ATTN_WIDTH = N_HEADS * HEAD_DIM
CONV_CH = 512
CONV_WIDTH = 31
CONV_HALO = 32
CONV_ROWS = 32
MOBA_BLOCK = 256
MOBA_TOPK = 3
PAGE_SIZE = 128
PAGES_PER_BLOCK = MOBA_BLOCK // PAGE_SIZE
D_FF = 2816
FFN_CHUNK = 256
EPS = 1e-6
NEG = -1e30
BIG = 1e30
SM_SCALE = HEAD_DIM ** -0.5
LOG2E = 1.4426950408889634

LANES = 128
SUBLANES = 8
BF16_ROWS = 16
HEADS_PER_LANE_TILE = LANES // HEAD_DIM
VMEM_LIMIT = 56 * 1024 * 1024

F32 = jnp.float32
BF16 = jnp.bfloat16
HIGHEST = lax.Precision.HIGHEST


def _sigmoid(x):
    return 1.0 / (1.0 + jnp.exp(-x))


def _silu(x):
    return x * _sigmoid(x)


def _dot(a, b, precision=None):
    return jnp.dot(a, b, precision=precision, preferred_element_type=F32)


def _dot_nt(a, b):
    return lax.dot_general(a, b, (((1,), (1,)), ((), ())), preferred_element_type=F32)


def _iota(shape, dim):
    return lax.broadcasted_iota(jnp.int32, shape, dim)


def _alibi_slope(h):
    return 2.0 ** -(h + 1)


def _mod_rms_norm(x, gain, scale, shift):
    ms = jnp.mean(x * x, axis=-1, keepdims=True)
    return x * lax.rsqrt(ms + EPS) * gain * (1.0 + scale) + shift


def _params(semantics):
    return pltpu.CompilerParams(dimension_semantics=semantics, vmem_limit_bytes=VMEM_LIMIT)


def _ada_kernel(cp_ref, cs_ref, w_ref, b_ref, mp_ref, ms_ref):
    w = w_ref[...]
    b = b_ref[...]
    mp_ref[...] = _dot(_silu(cp_ref[...]), w, HIGHEST) + b
    ms_ref[...] = _dot(_silu(cs_ref[...]), w, HIGHEST) + b


def _ada(c_prompt, c_sample, w_ada, b_ada):
    nb, ns = c_prompt.shape[0], c_sample.shape[0]
    n_out = w_ada.shape[1]
    tn = D_MODEL
    return pl.pallas_call(
        _ada_kernel,
        grid=(n_out // tn,),
        in_specs=[
            pl.BlockSpec((nb, D_MODEL), lambda j: (0, 0)),
            pl.BlockSpec((ns, D_MODEL), lambda j: (0, 0)),
            pl.BlockSpec((D_MODEL, tn), lambda j: (0, j)),
            pl.BlockSpec((1, tn), lambda j: (0, j)),
        ],
        out_specs=[
            pl.BlockSpec((nb, tn), lambda j: (0, j)),
            pl.BlockSpec((ns, tn), lambda j: (0, j)),
        ],
        out_shape=[jax.ShapeDtypeStruct((nb, n_out), F32), jax.ShapeDtypeStruct((ns, n_out), F32)],
        compiler_params=_params(("arbitrary",)),
        name="ada_mod",
    )(c_prompt, c_sample, w_ada, b_ada.reshape(1, n_out))


def _head_rms_norm(t, gain, group_ones):
    sq = t * t
    hi = sq.astype(BF16)
    lo = (sq - hi.astype(F32)).astype(BF16)
    ss = _dot(hi, group_ones) + _dot(lo, group_ones)
    return t * lax.rsqrt(ss * (1.0 / HEAD_DIM) + EPS) * gain


def _inproj_kernel(x_ref, sh_ref, sc_ref, g1_ref, w_ref, gq_ref, gk_ref, *refs, prompt):
    if prompt:
        q_ref, k_ref, v_ref, kb_ref, vt_ref, u_ref, km_ref = refs
    else:
        wq32_ref, q_ref, k_ref, v_ref, u_ref = refs
    h = _mod_rms_norm(x_ref[0], g1_ref[...], sc_ref[0], sh_ref[0])
    hb = h.astype(BF16)

    def proj(c0, c1):
        return _dot(hb, w_ref[:, c0:c1])

    a, c = ATTN_WIDTH, CONV_CH
    row = _iota((a, a), 0)
    col = _iota((a, a), 1)
    group_ones = jnp.where((row >> 6) == (col >> 6), 1.0, 0.0).astype(BF16)

    q_pre = proj(0, a) if prompt else _dot(h, wq32_ref[...], HIGHEST)
    q = _head_rms_norm(q_pre, gq_ref[...], group_ones)
    k = _head_rms_norm(proj(a, 2 * a), gk_ref[...], group_ones)
    v = proj(2 * a, 3 * a)
    q_ref[0] = q
    k_ref[0] = k
    v_ref[0] = v
    glu_a = proj(3 * a, 3 * a + c)
    glu_b = proj(3 * a + c, 3 * a + 2 * c)
    u_ref[0] = glu_a * _sigmoid(glu_b)
    if prompt:
        for r in range(k.shape[0] // MOBA_BLOCK):
            rows = slice(r * MOBA_BLOCK, (r + 1) * MOBA_BLOCK)
            kb_ref[0, r] = k[rows].astype(BF16)
            vt_ref[0, r] = v[rows].T.astype(BF16)
            km_ref[0, r] = jnp.mean(k[rows], axis=0, keepdims=True)


def _inproj(x3, mod3, g_norm1, w_in_b, gq_t, gk_t, *, tm, prompt, wq32=None):
    g, s, _ = x3.shape
    extra = [] if prompt else [wq32]
    r = mod3.shape[1]
    nt = s // tm

    def mod_spec(chunk):
        if r == 1:
            return pl.BlockSpec((1, 1, D_MODEL), lambda b, t: (b, 0, chunk))
        return pl.BlockSpec((1, tm, D_MODEL), lambda b, t: (b, t, chunk))

    tok = lambda width: pl.BlockSpec((1, tm, width), lambda b, t: (b, t, 0))
    full = lambda shape: pl.BlockSpec(shape, lambda b, t: (0,) * len(shape))
    out_specs = [tok(ATTN_WIDTH), tok(ATTN_WIDTH), tok(ATTN_WIDTH)]
    out_shape = [jax.ShapeDtypeStruct((g, s, ATTN_WIDTH), F32)] * 3
    if prompt:
        nblk = tm // MOBA_BLOCK
        blk_spec = lambda rows, cols: pl.BlockSpec((1, nblk, rows, cols), lambda b, t: (b, t, 0, 0))
        out_specs += [blk_spec(MOBA_BLOCK, ATTN_WIDTH), blk_spec(ATTN_WIDTH, MOBA_BLOCK)]
        out_shape += [jax.ShapeDtypeStruct((g, s // MOBA_BLOCK, MOBA_BLOCK, ATTN_WIDTH), BF16),
                      jax.ShapeDtypeStruct((g, s // MOBA_BLOCK, ATTN_WIDTH, MOBA_BLOCK), BF16)]
    out_specs += [tok(CONV_CH)]
    out_shape += [jax.ShapeDtypeStruct((g, s, CONV_CH), F32)]
    if prompt:
        out_specs += [blk_spec(1, ATTN_WIDTH)]
        out_shape += [jax.ShapeDtypeStruct((g, s // MOBA_BLOCK, 1, ATTN_WIDTH), F32)]
    return pl.pallas_call(
        functools.partial(_inproj_kernel, prompt=prompt),
        grid=(g, nt),
        in_specs=[
            tok(D_MODEL),
            mod_spec(0),
            mod_spec(1),
            full((1, D_MODEL)),
            full(w_in_b.shape),
            full((1, ATTN_WIDTH)),
            full((1, ATTN_WIDTH)),
        ] + [full(w.shape) for w in extra],
        out_specs=out_specs,
        out_shape=out_shape,
        compiler_params=_params(("arbitrary", "arbitrary")),
        name="in_proj_prompt" if prompt else "in_proj_sample",
    )(x3, mod3, mod3, g_norm1.reshape(1, D_MODEL), w_in_b, gq_t, gk_t, *extra)


def _moba_keep_t(q_t, km, q_blk, keep_ref):
    nb = km.shape[0]
    assert nb == SUBLANES
    kmb = jnp.concatenate([km] * N_HEADS, axis=0)
    row = _iota(kmb.shape, 0)
    col = _iota(kmb.shape, 1)
    kmb = jnp.where((row >> 3) == (col >> 6), kmb, 0.0)
    gate_t = _dot(kmb, q_t, HIGHEST)
    n_idx = _iota((nb, q_t.shape[1]), 0)
    for h in range(N_HEADS):
        gate = gate_t[h * nb:(h + 1) * nb]
        beaten_by = jnp.zeros(gate.shape, jnp.int32)
        for s in range(1, nb):
            g_m = pltpu.roll(gate, s, axis=0)
            lower = n_idx >= s
            m_idx = jnp.where(lower, n_idx - s, n_idx - s + nb)
            beats = (g_m > gate) | (lower & (g_m == gate))
            beaten_by += jnp.where(beats & (m_idx < q_blk), 1, 0)
        keep = (n_idx < q_blk) & (beaten_by < MOBA_TOPK)
        keep_ref[h * nb:(h + 1) * nb, :] = jnp.where(keep, 1.0, 0.0)


def _attn_kernel(q_ref, kb_ref, vt_ref, km_ref, o_ref, qz_ref, keep_ref, bias_ref, m_ref, accl_ref, x_ref, p_ref):
    i = pl.program_id(1)
    nb = km_ref.shape[1]
    blk = MOBA_BLOCK
    q_t = q_ref[0].T
    _moba_keep_t(q_t, km_ref[0], i, keep_ref)

    key_r = _iota((blk, blk), 0)
    qry_c = _iota((blk, blk), 1)
    rel_t = (qry_c - key_r).astype(F32)
    causal = qry_c >= key_r
    qz_t = q_t * (SM_SCALE * LOG2E)
    pair_row = _iota((LANES, blk), 0)
    ones_rows = jnp.ones((BF16_ROWS, blk), BF16)
    for h in range(N_HEADS):
        hp, half = divmod(h, HEADS_PER_LANE_TILE)
        bias_ref[h] = rel_t * (-_alibi_slope(h) * LOG2E)
        qz_ref[h] = jnp.where((pair_row >> 6) == half, qz_t[hp * LANES:(hp + 1) * LANES], 0.0).astype(BF16)

    def scores(jb):
        for h in range(N_HEADS):
            hp = h // HEADS_PER_LANE_TILE
            x_ref[h] = _dot(kb_ref[0, jb, :, hp * LANES:(hp + 1) * LANES], qz_ref[h]) + bias_ref[h]

    def values(h, jb):
        return jnp.concatenate([vt_ref[0, jb, h * HEAD_DIM:(h + 1) * HEAD_DIM, :], ones_rows], axis=0)

    scores(i)
    for h in range(N_HEADS):
        x = jnp.where(causal, x_ref[h], NEG)
        m = jnp.max(x, axis=0, keepdims=True)
        p_ref[h] = jnp.exp2(x - m).astype(BF16)
        m_ref[h:h + 1, :] = m
    for h in range(N_HEADS):
        accl_ref[h] = _dot(values(h, i), p_ref[h])

    def past_block(jb, carry):
        offset = ((i - jb) * blk).astype(F32)
        scores(jb)
        alphas = []
        for h in range(N_HEADS):
            x = x_ref[h]
            c_j = offset * (_alibi_slope(h) * LOG2E)
            kept = keep_ref[pl.ds(h * nb + jb, 1), :] > 0.5
            m_old = m_ref[h:h + 1, :]
            m_new = jnp.maximum(m_old, jnp.where(kept, jnp.max(x, axis=0, keepdims=True) - c_j, NEG))
            alphas.append(jnp.exp2(m_old - m_new))
            p_ref[h] = jnp.exp2(x - jnp.where(kept, m_new + c_j, BIG)).astype(BF16)
            m_ref[h:h + 1, :] = m_new
        for h in range(N_HEADS):
            accl_ref[h] = alphas[h] * accl_ref[h] + _dot(values(h, jb), p_ref[h])
        return carry

    lax.fori_loop(0, i, past_block, 0)

    outs = [accl_ref[h, 0:HEAD_DIM, :] / accl_ref[h, HEAD_DIM:HEAD_DIM + 1, :] for h in range(N_HEADS)]
    o_ref[0] = jnp.concatenate(outs, axis=0).T.astype(BF16)


def _attn(q, kb, vt, km):
    b, s, _ = q.shape
    nq = s // MOBA_BLOCK
    return pl.pallas_call(
        _attn_kernel,
        grid=(b, nq),
        in_specs=[
            pl.BlockSpec((1, MOBA_BLOCK, ATTN_WIDTH), lambda bi, i: (bi, i, 0)),
            pl.BlockSpec((1, nq, MOBA_BLOCK, ATTN_WIDTH), lambda bi, i: (bi, 0, 0, 0)),
            pl.BlockSpec((1, nq, ATTN_WIDTH, MOBA_BLOCK), lambda bi, i: (bi, 0, 0, 0)),
            pl.BlockSpec((1, nq, ATTN_WIDTH), lambda bi, i: (bi, 0, 0)),
        ],
        out_specs=pl.BlockSpec((1, MOBA_BLOCK, ATTN_WIDTH), lambda bi, i: (bi, i, 0)),
        out_shape=jax.ShapeDtypeStruct((b, s, ATTN_WIDTH), BF16),
        scratch_shapes=[
            pltpu.VMEM((N_HEADS, LANES, MOBA_BLOCK), BF16),
            pltpu.VMEM((N_HEADS * nq, MOBA_BLOCK), F32),
            pltpu.VMEM((N_HEADS, MOBA_BLOCK, MOBA_BLOCK), F32),
            pltpu.VMEM((N_HEADS, MOBA_BLOCK), F32),
            pltpu.VMEM((N_HEADS, HEAD_DIM + BF16_ROWS, MOBA_BLOCK), F32),
            pltpu.VMEM((N_HEADS, MOBA_BLOCK, MOBA_BLOCK), F32),
            pltpu.VMEM((N_HEADS, MOBA_BLOCK, MOBA_BLOCK), BF16),
        ],
        compiler_params=_params(("arbitrary", "arbitrary")),
        name="moba_attn_prompt",
    )(q, kb, vt, km)


def _mix_head(attn_b, x, sh1, sc1, g1, wg_ref, wap_ref):
    hb = _mod_rms_norm(x, g1, sc1, sh1).astype(BF16)
    sga = _sigmoid(_dot(hb, wg_ref[:, :D_MODEL]))
    sgc = _sigmoid(_dot(hb, wg_ref[:, D_MODEL:]))
    return sga * _dot(attn_b, wap_ref[...]), sgc


def _mix_tail(cv, gated_attn, sgc, x, gt1, gln, bln, wcp_ref, wo_ref):
    mu = jnp.mean(cv, axis=-1, keepdims=True)
    cen = cv - mu
    var = jnp.mean(cen * cen, axis=-1, keepdims=True)
    y = cen * lax.rsqrt(var + EPS) * gln + bln
    cvp = _dot(_silu(y).astype(BF16), wcp_ref[...])
    mix = _dot((gated_attn + sgc * cvp).astype(BF16), wo_ref[...])
    return x + gt1 * mix


def _mix_prompt_kernel(x_ref, attn_ref, u_ref, halo_ref, sh1_ref, sc1_ref, gt1_ref, g1_ref, wdw_ref, bdw_ref,
                       gln_ref, bln_ref, wg_ref, wap_ref, wcp_ref, wo_ref, x1_ref, ext_ref, cv_ref):
    t = pl.program_id(1)
    tm = u_ref.shape[1]
    x = x_ref[0]
    gated_attn, sgc = _mix_head(attn_ref[0], x, sh1_ref[0], sc1_ref[0], g1_ref[...], wg_ref, wap_ref)
    ext_ref[0, 0:CONV_HALO, :] = jnp.where(t > 0, halo_ref[0], 0.0)
    ext_ref[0, CONV_HALO:, :] = u_ref[0]
    shifted_rows = tm + CONV_HALO - SUBLANES
    for r in range(1, SUBLANES):
        ext_ref[r, 0:shifted_rows, :] = ext_ref[0, r:r + shifted_rows, :]
    lead = CONV_HALO - (CONV_WIDTH - 1)
    for r0 in range(0, tm, CONV_ROWS):
        acc = jnp.zeros((CONV_ROWS // SUBLANES, SUBLANES, CONV_CH), F32)
        for w in range(CONV_WIDTH):
            r = (lead + w) % SUBLANES
            a = lead + w - r + r0
            acc = acc + ext_ref[r, a:a + CONV_ROWS, :].reshape(acc.shape) * wdw_ref[w]
        cv_ref[r0:r0 + CONV_ROWS, :] = acc.reshape(CONV_ROWS, CONV_CH) + bdw_ref[...]
    x1_ref[0] = _mix_tail(cv_ref[...], gated_attn, sgc, x, gt1_ref[0], gln_ref[...], bln_ref[...], wcp_ref, wo_ref)


def _mix_prompt(x3, attn, u, mod3, g_norm1, wdw, bdw, gln, bln, wg_b, wap_b, wcp_b, wo_b, *, tm):
    b, s, _ = x3.shape
    nt = s // tm
    halo_per_tile = tm // CONV_HALO
    tok = lambda width: pl.BlockSpec((1, tm, width), lambda bi, t: (bi, t, 0))
    full = lambda shape: pl.BlockSpec(shape, lambda bi, t: (0,) * len(shape))
    mod_spec = lambda chunk: pl.BlockSpec((1, 1, D_MODEL), lambda bi, t: (bi, 0, chunk))
    return pl.pallas_call(
        _mix_prompt_kernel,
        grid=(b, nt),
        in_specs=[
            tok(D_MODEL), tok(ATTN_WIDTH), tok(CONV_CH),
            pl.BlockSpec((1, CONV_HALO, CONV_CH), lambda bi, t: (bi, jnp.maximum(t * halo_per_tile - 1, 0), 0)),
            mod_spec(0), mod_spec(1), mod_spec(2),
            full((1, D_MODEL)),
            full(wdw.shape), full((1, CONV_CH)), full((1, CONV_CH)), full((1, CONV_CH)),
            full(wg_b.shape), full(wap_b.shape), full(wcp_b.shape), full(wo_b.shape),
        ],
        out_specs=tok(D_MODEL),
        out_shape=jax.ShapeDtypeStruct((b, s, D_MODEL), F32),
        scratch_shapes=[pltpu.VMEM((SUBLANES, CONV_HALO + tm, CONV_CH), F32), pltpu.VMEM((tm, CONV_CH), F32)],
        compiler_params=_params(("arbitrary", "arbitrary")),
        name="mix_prompt",
    )(x3, attn, u, u, mod3, mod3, mod3, g_norm1.reshape(1, D_MODEL), wdw, bdw, gln, bln, wg_b, wap_b, wcp_b, wo_b)


def _mix_sample_kernel(x_ref, attn_ref, u_ref, st_ref, sh1_ref, sc1_ref, gt1_ref, g1_ref, wdw_ref, bdw_ref,
                       gln_ref, bln_ref, wg_ref, wap_ref, wcp_ref, wo_ref, x1_ref, st_out_ref):
    u = u_ref[...]
    past = CONV_WIDTH - 1
    cv = u * wdw_ref[past:past + 1, :] + bdw_ref[...]
    for w in range(past):
        cv = cv + st_ref[w] * wdw_ref[w:w + 1, :]
    for w in range(1, past):
        st_out_ref[w - 1] = st_ref[w]
    st_out_ref[past - 1] = u
    x = x_ref[...]
    gated_attn, sgc = _mix_head(attn_ref[...].astype(BF16), x, sh1_ref[...], sc1_ref[...], g1_ref[...], wg_ref, wap_ref)
    x1_ref[...] = _mix_tail(cv, gated_attn, sgc, x, gt1_ref[...], gln_ref[...], bln_ref[...], wcp_ref, wo_ref)


def _mix_sample(x2, attn, u, st_t, mod_s, g_norm1, wdw, bdw, gln, bln, wg_b, wap_b, wcp_b, wo_b):
    n = x2.shape[0]
    full = lambda shape: pl.BlockSpec(shape, lambda j: (0,) * len(shape))
    mod_spec = lambda chunk: pl.BlockSpec((n, D_MODEL), lambda j: (0, chunk))
    return pl.pallas_call(
        _mix_sample_kernel,
        grid=(1,),
        in_specs=[
            full(x2.shape), full(attn.shape), full(u.shape), full(st_t.shape),
            mod_spec(0), mod_spec(1), mod_spec(2),
            full((1, D_MODEL)),
            full(wdw.shape), full((1, CONV_CH)), full((1, CONV_CH)), full((1, CONV_CH)),
            full(wg_b.shape), full(wap_b.shape), full(wcp_b.shape), full(wo_b.shape),
        ],
        out_specs=[full(x2.shape), full(st_t.shape)],
        out_shape=[jax.ShapeDtypeStruct(x2.shape, F32), jax.ShapeDtypeStruct(st_t.shape, F32)],
        compiler_params=_params(("arbitrary",)),
        name="mix_sample",
    )(x2, attn, u, st_t, mod_s, mod_s, mod_s, g_norm1.reshape(1, D_MODEL), wdw, bdw, gln, bln, wg_b, wap_b, wcp_b, wo_b)


def _ffn_kernel(x_ref, sh_ref, sc_ref, gt_ref, g2_ref, wi_ref, wo_ref, y_ref):
    x = x_ref[0]
    hb = _mod_rms_norm(x, g2_ref[...], sc_ref[0], sh_ref[0]).astype(BF16)
    acc = jnp.zeros(x.shape, F32)
    for c in range(D_FF // FFN_CHUNK):
        c0 = c * FFN_CHUNK
        f_g = _dot(hb, wi_ref[:, c0:c0 + FFN_CHUNK])
        f_u = _dot(hb, wi_ref[:, D_FF + c0:D_FF + c0 + FFN_CHUNK])
        acc = acc + _dot((_silu(f_g) * f_u).astype(BF16), wo_ref[c0:c0 + FFN_CHUNK, :])
    y_ref[0] = x + gt_ref[0] * acc


def _ffn(x3, mod3, g_norm2, wi_b, wo_b, *, tm, name):
    g, s, _ = x3.shape
    r = mod3.shape[1]
    nt = s // tm

    def mod_spec(chunk):
        if r == 1:
            return pl.BlockSpec((1, 1, D_MODEL), lambda b, t: (b, 0, chunk))
        return pl.BlockSpec((1, tm, D_MODEL), lambda b, t: (b, t, chunk))

    tok = pl.BlockSpec((1, tm, D_MODEL), lambda b, t: (b, t, 0))
    full = lambda shape: pl.BlockSpec(shape, lambda b, t: (0,) * len(shape))
    return pl.pallas_call(
        _ffn_kernel,
        grid=(g, nt),
        in_specs=[tok, mod_spec(3), mod_spec(4), mod_spec(5), full((1, D_MODEL)), full(wi_b.shape), full(wo_b.shape)],
        out_specs=tok,
        out_shape=jax.ShapeDtypeStruct(x3.shape, F32),
        compiler_params=_params(("arbitrary", "arbitrary")),
        name=name,
    )(x3, mod3, mod3, mod3, g_norm2.reshape(1, D_MODEL), wi_b, wo_b)


GATE_PAGES_PER_STEP = 16


def _gate_kernel(pt_ref, q_ref, *refs):
    del pt_ref
    k_refs = refs[:GATE_PAGES_PER_STEP]
    idx_ref = refs[GATE_PAGES_PER_STEP]
    qcol_ref, gate_ref = refs[GATE_PAGES_PER_STEP + 1:]
    step = pl.program_id(1)
    blocks_per_step = GATE_PAGES_PER_STEP // PAGES_PER_BLOCK

    @pl.when(step == 0)
    def _():
        qcol_ref[...] = jnp.broadcast_to(q_ref[0], (LANES, ATTN_WIDTH)).T
        gate_ref[...] = jnp.zeros(gate_ref.shape, F32)

    qcol = qcol_ref[...]
    sub = _iota((N_HEADS, LANES), 0)
    lane = _iota((N_HEADS, LANES), 1)
    gate = gate_ref[...]
    for blk in range(blocks_per_step):
        kt = k_refs[blk * PAGES_PER_BLOCK][0]
        for pg in range(1, PAGES_PER_BLOCK):
            kt = kt + k_refs[blk * PAGES_PER_BLOCK + pg][0]
        prod = kt.reshape(ATTN_WIDTH, PAGE_SIZE) * qcol
        score = jnp.zeros((N_HEADS, LANES), F32)
        for h in range(N_HEADS):
            s_h = jnp.sum(prod[h * HEAD_DIM:(h + 1) * HEAD_DIM], axis=0, keepdims=True)
            score = jnp.where(sub == h, s_h, score)
        g_blk = jnp.sum(score, axis=1, keepdims=True) * (1.0 / MOBA_BLOCK)
        gate = jnp.where(lane == step * blocks_per_step + blk, g_blk, gate)
    gate_ref[...] = gate

    @pl.when(step == pl.num_programs(1) - 1)
    def _():
        n_blocks = pl.num_programs(1) * blocks_per_step
        g = jnp.where(lane < n_blocks, gate, -jnp.inf)
        out = jnp.zeros((N_HEADS, LANES), jnp.int32)
        for r in range(MOBA_TOPK):
            best = jnp.max(g, axis=1, keepdims=True)
            pick = jnp.min(jnp.where(g == best, lane, LANES), axis=1, keepdims=True)
            out = jnp.where(lane == r, pick, out)
            g = jnp.where(lane == pick, -jnp.inf, g)
        idx_ref[0] = out


def _gate_topk(page_table, q_s3, cache_kt):
    nseq, n_pages = page_table.shape
    steps = n_pages // GATE_PAGES_PER_STEP

    def page_spec(r):
        return pl.BlockSpec((1, N_HEADS, HEAD_DIM, PAGE_SIZE),
                            lambda b, s, pt: (pt[b, s * GATE_PAGES_PER_STEP + r], 0, 0, 0))

    grid_spec = pltpu.PrefetchScalarGridSpec(
        num_scalar_prefetch=1,
        grid=(nseq, steps),
        in_specs=[pl.BlockSpec((1, 1, ATTN_WIDTH), lambda b, s, pt: (b, 0, 0))]
        + [page_spec(r) for r in range(GATE_PAGES_PER_STEP)],
        out_specs=pl.BlockSpec((1, N_HEADS, LANES), lambda b, s, pt: (b, 0, 0)),
        scratch_shapes=[pltpu.VMEM((ATTN_WIDTH, LANES), F32), pltpu.VMEM((N_HEADS, LANES), F32)],
    )
    return pl.pallas_call(
        _gate_kernel,
        grid_spec=grid_spec,
        out_shape=jax.ShapeDtypeStruct((nseq, N_HEADS, LANES), jnp.int32),
        compiler_params=_params(("arbitrary", "arbitrary")),
        name="moba_gate_sample",
    )(page_table, q_s3, *([cache_kt] * GATE_PAGES_PER_STEP))


SEL_PAGES = MOBA_TOPK * PAGES_PER_BLOCK
SATTN_HEADS_PER_STEP = 4


def _sattn_kernel(pt_ref, idx_ref, q_ref, kn_ref, vn_ref, *refs, past_len):
    del pt_ref
    n_pg = SATTN_HEADS_PER_STEP * SEL_PAGES
    k_refs = refs[:n_pg]
    v_refs = refs[n_pg:2 * n_pg]
    o_ref = refs[2 * n_pg]
    b = pl.program_id(0)
    hg = pl.program_id(1)
    lane = _iota((SUBLANES, PAGE_SIZE), 1)
    for hh in range(SATTN_HEADS_PER_STEP):
        h = hg * SATTN_HEADS_PER_STEP + hh
        q = q_ref[0, hh] * SM_SCALE
        q8 = jnp.broadcast_to(q, (SUBLANES, HEAD_DIM)).astype(BF16)
        k_cat = jnp.concatenate([k_refs[hh * SEL_PAGES + j][0, 0] for j in range(SEL_PAGES)], axis=1)
        v_cat = jnp.concatenate([v_refs[hh * SEL_PAGES + j][0, 0] for j in range(SEL_PAGES)], axis=1)
        pos = []
        for r in range(MOBA_TOPK):
            blk = idx_ref[(b * N_HEADS + h) * MOBA_TOPK + r]
            pos += [blk * MOBA_BLOCK + pg * PAGE_SIZE + lane for pg in range(PAGES_PER_BLOCK)]
        dist = (past_len - jnp.concatenate(pos, axis=1)).astype(F32)
        slope = lax.bitcast_convert_type(jnp.full(dist.shape, 126, jnp.int32) - h << 23, F32)
        s = _dot(q8, k_cat.astype(BF16)) - slope * dist
        s_own = jnp.sum(q * kn_ref[0, hh], axis=1, keepdims=True)
        m = jnp.maximum(jnp.max(s[0:1], axis=1, keepdims=True), s_own)
        p = jnp.exp(s - m)
        p_own = jnp.exp(s_own - m)
        l = jnp.sum(p[0:1], axis=1, keepdims=True) + p_own
        acc = _dot_nt(p.astype(BF16), v_cat.astype(BF16))[0:1] + p_own * vn_ref[0, hh]
        o_ref[0, hh] = acc / l


def _sattn(page_table, idx_flat, q4, kn4, vn4, cache_kt, cache_vt):
    nseq, n_pages = page_table.shape
    past_len = n_pages * PAGE_SIZE
    hps = SATTN_HEADS_PER_STEP

    def page_spec(hh, r, pg):
        def index_map(b, hg, pt, idx):
            h = hg * hps + hh
            return (pt[b, idx[(b * N_HEADS + h) * MOBA_TOPK + r] * PAGES_PER_BLOCK + pg], h, 0, 0)
        return pl.BlockSpec((1, 1, HEAD_DIM, PAGE_SIZE), index_map)

    tok = pl.BlockSpec((1, hps, 1, HEAD_DIM), lambda b, hg, pt, idx: (b, hg, 0, 0))
    pages = [page_spec(hh, r, pg) for hh in range(hps) for r in range(MOBA_TOPK) for pg in range(PAGES_PER_BLOCK)]
    grid_spec = pltpu.PrefetchScalarGridSpec(
        num_scalar_prefetch=2,
        grid=(nseq, N_HEADS // hps),
        in_specs=[tok, tok, tok] + pages + pages,
        out_specs=tok,
    )
    return pl.pallas_call(
        functools.partial(_sattn_kernel, past_len=past_len),
        grid_spec=grid_spec,
        out_shape=jax.ShapeDtypeStruct((nseq, N_HEADS, 1, HEAD_DIM), F32),
        compiler_params=_params(("arbitrary", "arbitrary")),
        name="moba_attn_sample",
    )(page_table, idx_flat, q4, kn4, vn4, *([cache_kt] * len(pages)), *([cache_vt] * len(pages)))


def kernel(x_prompt, x_sample, cache_k, cache_v, state_conv, page_table, c_prompt, c_sample, w_ada, b_ada, g_norm1, w_in, g_q, g_k, w_attn_proj, w_dwconv, b_dwconv, g_conv_ln, b_conv_ln, w_conv_proj, w_out, g_norm2, w_ffn_in, w_ffn_out):
    nb, seq, _ = x_prompt.shape
    ns = x_sample.shape[0]

    n_qkvu = 3 * ATTN_WIDTH + 2 * CONV_CH
    w_in_b = w_in[:, :n_qkvu].astype(BF16)
    wg_b = w_in[:, n_qkvu:].astype(BF16)
    wap_b = w_attn_proj.astype(BF16)
    wcp_b = w_conv_proj.astype(BF16)
    wo_b = w_out.astype(BF16)
    wfi_b = w_ffn_in.astype(BF16)
    wfo_b = w_ffn_out.astype(BF16)
    gq_t = jnp.tile(g_q, N_HEADS).reshape(1, ATTN_WIDTH)
    gk_t = jnp.tile(g_k, N_HEADS).reshape(1, ATTN_WIDTH)
    wdw = w_dwconv.reshape(CONV_WIDTH, CONV_CH)
    bdw = b_dwconv.reshape(1, CONV_CH)
    gln = g_conv_ln.reshape(1, CONV_CH)
    bln = b_conv_ln.reshape(1, CONV_CH)

    mod_p, mod_s = _ada(c_prompt, c_sample, w_ada, b_ada)
    mod_p3 = mod_p.reshape(nb, 1, 6 * D_MODEL)
    mod_s3 = mod_s.reshape(1, ns, 6 * D_MODEL)

    q, k, v, kb, vt, u, km = _inproj(x_prompt, mod_p3, g_norm1, w_in_b, gq_t, gk_t, tm=256, prompt=True)
    attn = _attn(q, kb, vt, km.reshape(nb, seq // MOBA_BLOCK, ATTN_WIDTH))
    wdw8 = jnp.broadcast_to(wdw[:, None, :], (CONV_WIDTH, SUBLANES, CONV_CH))
    x1 = _mix_prompt(x_prompt, attn, u, mod_p3, g_norm1, wdw8, bdw, gln, bln, wg_b, wap_b, wcp_b, wo_b, tm=256)
    y_prompt = _ffn(x1, mod_p3, g_norm2, wfi_b, wfo_b, tm=256, name="ffn_prompt")
    k_prompt = k.reshape(nb, seq, N_HEADS, HEAD_DIM)
    v_prompt = v.reshape(nb, seq, N_HEADS, HEAD_DIM)
    conv_prompt = u[:, seq - (CONV_WIDTH - 1):, :]

    xs3 = x_sample.reshape(1, ns, D_MODEL)
    qs, ks, vs, us = _inproj(xs3, mod_s3, g_norm1, w_in_b, gq_t, gk_t, tm=ns, prompt=False,
                                           wq32=w_in[:, :ATTN_WIDTH])
    cache_kt = jnp.transpose(cache_k, (0, 2, 3, 1))
    cache_vt = jnp.transpose(cache_v, (0, 2, 3, 1))
    idx = _gate_topk(page_table, qs.reshape(ns, 1, ATTN_WIDTH), cache_kt)
    idx_flat = idx[:, :, :MOBA_TOPK].reshape(-1)
    head4 = lambda t: t.reshape(ns, N_HEADS, 1, HEAD_DIM)
    attn_s = _sattn(page_table, idx_flat, head4(qs), head4(ks), head4(vs), cache_kt, cache_vt)
    x1_s, st_new = _mix_sample(
        xs3[0], attn_s.reshape(ns, ATTN_WIDTH), us[0], jnp.transpose(state_conv, (1, 0, 2)),
        mod_s, g_norm1, wdw, bdw, gln, bln, wg_b, wap_b, wcp_b, wo_b)
    y_sample = _ffn(x1_s.reshape(1, ns, D_MODEL), mod_s3, g_norm2, wfi_b, wfo_b, tm=ns, name="ffn_sample")

    return (y_prompt, y_sample.reshape(ns, 1, D_MODEL), k_prompt, v_prompt, conv_prompt,
            ks.reshape(ns, 1, N_HEADS, HEAD_DIM), vs.reshape(ns, 1, N_HEADS, HEAD_DIM),
            jnp.transpose(st_new, (1, 0, 2)))
```

```python
import functools

import jax
import jax.numpy as jnp
from jax import lax
from jax.experimental import pallas as pl
from jax.experimental.pallas import tpu as pltpu

D_MODEL = 1024
N_HEADS = 8
HEAD_DIM = 64
ATTN_WIDTH = N_HEADS * HEAD_DIM
CONV_CH = 512
CONV_WIDTH = 31
CONV_HALO = 32
CONV_ROWS = 32
MOBA_BLOCK = 256
MOBA_TOPK = 3
PAGE_SIZE = 128
PAGES_PER_BLOCK = MOBA_BLOCK // PAGE_SIZE
D_FF = 2816
FFN_CHUNK = 256
EPS = 1e-6
NEG = -1e30
BIG = 1e30
SM_SCALE = HEAD_DIM ** -0.5
LOG2E = 1.4426950408889634

LANES = 128
SUBLANES = 8
BF16_ROWS = 16
HEADS_PER_LANE_TILE = LANES // HEAD_DIM
VMEM_LIMIT = 56 * 1024 * 1024

F32 = jnp.float32
BF16 = jnp.bfloat16
HIGHEST = lax.Precision.HIGHEST


def _sigmoid(x):
    return 1.0 / (1.0 + jnp.exp(-x))


def _silu(x):
    return x * _sigmoid(x)


def _dot(a, b, precision=None):
    return jnp.dot(a, b, precision=precision, preferred_element_type=F32)


def _dot_nt(a, b):
    return lax.dot_general(a, b, (((1,), (1,)), ((), ())), preferred_element_type=F32)


def _iota(shape, dim):
    return lax.broadcasted_iota(jnp.int32, shape, dim)


def _alibi_slope(h):
    return 2.0 ** -(h + 1)


def _mod_rms_norm(x, gain, scale, shift):
    ms = jnp.mean(x * x, axis=-1, keepdims=True)
    return x * lax.rsqrt(ms + EPS) * gain * (1.0 + scale) + shift


def _resident(shape):
    return pl.BlockSpec(shape, lambda *_: (0,) * len(shape), pipeline_mode=pl.Buffered(1))


def _params(semantics):
    return pltpu.CompilerParams(dimension_semantics=semantics, vmem_limit_bytes=VMEM_LIMIT)


def _ada_kernel(cp_ref, cs_ref, w_ref, b_ref, mp_ref, ms_ref):
    w = w_ref[...]
    b = b_ref[...]
    mp_ref[...] = _dot(_silu(cp_ref[...]), w, HIGHEST) + b
    ms_ref[...] = _dot(_silu(cs_ref[...]), w, HIGHEST) + b


def _ada(c_prompt, c_sample, w_ada, b_ada):
    nb, ns = c_prompt.shape[0], c_sample.shape[0]
    n_out = w_ada.shape[1]
    tn = D_MODEL
    return pl.pallas_call(
        _ada_kernel,
        grid=(n_out // tn,),
        in_specs=[
            pl.BlockSpec((nb, D_MODEL), lambda j: (0, 0)),
            pl.BlockSpec((ns, D_MODEL), lambda j: (0, 0)),
            pl.BlockSpec((D_MODEL, tn), lambda j: (0, j)),
            pl.BlockSpec((1, tn), lambda j: (0, j)),
        ],
        out_specs=[
            pl.BlockSpec((nb, tn), lambda j: (0, j)),
            pl.BlockSpec((ns, tn), lambda j: (0, j)),
        ],
        out_shape=[jax.ShapeDtypeStruct((nb, n_out), F32), jax.ShapeDtypeStruct((ns, n_out), F32)],
        compiler_params=_params(("arbitrary",)),
        name="ada_mod",
    )(c_prompt, c_sample, w_ada, b_ada.reshape(1, n_out))


def _head_rms_norm(t, gain, group_ones):
    sq = t * t
    hi = sq.astype(BF16)
    lo = (sq - hi.astype(F32)).astype(BF16)
    ss = _dot(hi, group_ones) + _dot(lo, group_ones)
    return t * lax.rsqrt(ss * (1.0 / HEAD_DIM) + EPS) * gain


def _inproj_kernel(*refs, host):
    if host is not None:
        refs = refs[1:]
        x_ref, sh_ref, sc_ref, g1_ref, w_ref, gq_ref, gk_ref = refs[:7]
        host_refs = refs[7:8 + HOST_PAGES]
        q_ref, k_ref, v_ref, kb_ref, vt_ref, u_ref, km_ref, gate_ref = refs[8 + HOST_PAGES:]
    else:
        x_ref, sh_ref, sc_ref, g1_ref, w_ref, gq_ref, gk_ref, wq32_ref, q_ref, k_ref, v_ref, u_ref = refs
    prompt = host is not None
    h = _mod_rms_norm(x_ref[0], g1_ref[...], sc_ref[0], sh_ref[0])
    hb = h.astype(BF16)

    def proj(c0, c1):
        return _dot(hb, w_ref[:, c0:c1])

    a, c = ATTN_WIDTH, CONV_CH
    row = _iota((a, a), 0)
    col = _iota((a, a), 1)
    group_ones = jnp.where((row >> 6) == (col >> 6), 1.0, 0.0).astype(BF16)

    q_pre = proj(0, a) if prompt else _dot(h, wq32_ref[...], HIGHEST)
    q = _head_rms_norm(q_pre, gq_ref[...], group_ones)
    k = _head_rms_norm(proj(a, 2 * a), gk_ref[...], group_ones)
    v = proj(2 * a, 3 * a)
    q_ref[0] = q
    k_ref[0] = k
    v_ref[0] = v
    glu_a = proj(3 * a, 3 * a + c)
    glu_b = proj(3 * a + c, 3 * a + 2 * c)
    u_ref[0] = glu_a * _sigmoid(glu_b)
    if prompt:
        for r in range(k.shape[0] // MOBA_BLOCK):
            rows = slice(r * MOBA_BLOCK, (r + 1) * MOBA_BLOCK)
            kb_ref[0, r] = k[rows].astype(BF16)
            vt_ref[0, r] = v[rows].T.astype(BF16)
            km_ref[0, r] = jnp.mean(k[rows], axis=0, keepdims=True)
        _host_gate(host_refs, gate_ref, host)


def _inproj(x3, mod3, g_norm1, w_in_b, gq_t, gk_t, *, tm, wq32=None, gate_host=None):
    g, s, _ = x3.shape
    r = mod3.shape[1]
    nt = s // tm
    prompt = gate_host is not None
    host = _GateHost(*gate_host, grid=(g, nt)) if prompt else None

    def mod_spec(chunk):
        if r == 1:
            return pl.BlockSpec((1, 1, D_MODEL), lambda b, t, *_: (b, 0, chunk))
        return pl.BlockSpec((1, tm, D_MODEL), lambda b, t, *_: (b, t, chunk))

    tok = lambda width: pl.BlockSpec((1, tm, width), lambda b, t, *_: (b, t, 0))
    in_specs = [tok(D_MODEL), mod_spec(0), mod_spec(1),
                _resident((1, D_MODEL)), _resident(w_in_b.shape), _resident((1, ATTN_WIDTH)), _resident((1, ATTN_WIDTH))]
    operands = [x3, mod3, mod3, g_norm1.reshape(1, D_MODEL), w_in_b, gq_t, gk_t]
    out_specs = [tok(ATTN_WIDTH), tok(ATTN_WIDTH), tok(ATTN_WIDTH)]
    out_shape = [jax.ShapeDtypeStruct((g, s, ATTN_WIDTH), F32)] * 3
    if prompt:
        nblk = tm // MOBA_BLOCK
        blk_spec = lambda rows, cols: pl.BlockSpec((1, nblk, rows, cols), lambda b, t, *_: (b, t, 0, 0))
        in_specs += host.in_specs()
        operands += host.operands()
        out_specs += [blk_spec(MOBA_BLOCK, ATTN_WIDTH), blk_spec(ATTN_WIDTH, MOBA_BLOCK), tok(CONV_CH),
                      blk_spec(1, ATTN_WIDTH), host.out_spec()]
        out_shape += [jax.ShapeDtypeStruct((g, s // MOBA_BLOCK, MOBA_BLOCK, ATTN_WIDTH), BF16),
                      jax.ShapeDtypeStruct((g, s // MOBA_BLOCK, ATTN_WIDTH, MOBA_BLOCK), BF16),
                      jax.ShapeDtypeStruct((g, s, CONV_CH), F32),
                      jax.ShapeDtypeStruct((g, s // MOBA_BLOCK, 1, ATTN_WIDTH), F32),
                      host.out_shape()]
        prefetch = [host.page_table]
    else:
        in_specs += [_resident(wq32.shape)]
        operands += [wq32]
        out_specs += [tok(CONV_CH)]
        out_shape += [jax.ShapeDtypeStruct((g, s, CONV_CH), F32)]
        prefetch = []
    grid_spec = pltpu.PrefetchScalarGridSpec(
        num_scalar_prefetch=len(prefetch), grid=(g, nt), in_specs=in_specs, out_specs=out_specs)
    return pl.pallas_call(
        functools.partial(_inproj_kernel, host=host),
        grid_spec=grid_spec,
        out_shape=out_shape,
        compiler_params=_params(("arbitrary", "arbitrary")),
        name="in_proj_prompt" if prompt else "in_proj_sample",
    )(*prefetch, *operands)


def _moba_keep_t(q_t, km, q_blk, keep_ref):
    nb = km.shape[0]
    assert nb == SUBLANES
    kmb = jnp.concatenate([km] * N_HEADS, axis=0)
    row = _iota(kmb.shape, 0)
    col = _iota(kmb.shape, 1)
    kmb = jnp.where((row >> 3) == (col >> 6), kmb, 0.0)
    gate_t = _dot(kmb, q_t, HIGHEST)
    n_idx = _iota((nb, q_t.shape[1]), 0)
    for h in range(N_HEADS):
        gate = gate_t[h * nb:(h + 1) * nb]
        beaten_by = jnp.zeros(gate.shape, jnp.int32)
        for s in range(1, nb):
            g_m = pltpu.roll(gate, s, axis=0)
            lower = n_idx >= s
            m_idx = jnp.where(lower, n_idx - s, n_idx - s + nb)
            beats = (g_m > gate) | (lower & (g_m == gate))
            beaten_by += jnp.where(beats & (m_idx < q_blk), 1, 0)
        keep = (n_idx < q_blk) & (beaten_by < MOBA_TOPK)
        keep_ref[h * nb:(h + 1) * nb, :] = jnp.where(keep, 1.0, 0.0)


def _attn_kernel(q_ref, kb_ref, vt_ref, km_ref, o_ref, qz_ref, keep_ref, bias_ref, m_ref, accl_ref, x_ref, p_ref):
    i = pl.program_id(1)
    nb = km_ref.shape[1]
    blk = MOBA_BLOCK
    q_t = q_ref[0].T
    _moba_keep_t(q_t, km_ref[0], i, keep_ref)

    key_r = _iota((blk, blk), 0)
    qry_c = _iota((blk, blk), 1)
    rel_t = (qry_c - key_r).astype(F32)
    causal = qry_c >= key_r
    qz_t = q_t * (SM_SCALE * LOG2E)
    pair_row = _iota((LANES, blk), 0)
    ones_rows = jnp.ones((BF16_ROWS, blk), BF16)
    for h in range(N_HEADS):
        hp, half = divmod(h, HEADS_PER_LANE_TILE)
        bias_ref[h] = rel_t * (-_alibi_slope(h) * LOG2E)
        qz_ref[h] = jnp.where((pair_row >> 6) == half, qz_t[hp * LANES:(hp + 1) * LANES], 0.0).astype(BF16)

    def scores(jb):
        for h in range(N_HEADS):
            hp = h // HEADS_PER_LANE_TILE
            x_ref[h] = _dot(kb_ref[0, jb, :, hp * LANES:(hp + 1) * LANES], qz_ref[h]) + bias_ref[h]

    def values(h, jb):
        return jnp.concatenate([vt_ref[0, jb, h * HEAD_DIM:(h + 1) * HEAD_DIM, :], ones_rows], axis=0)

    scores(i)
    for h in range(N_HEADS):
        x = jnp.where(causal, x_ref[h], NEG)
        m = jnp.max(x, axis=0, keepdims=True)
        p_ref[h] = jnp.exp2(x - m).astype(BF16)
        m_ref[h:h + 1, :] = m
    for h in range(N_HEADS):
        accl_ref[h] = _dot(values(h, i), p_ref[h])

    def past_block(jb, carry):
        offset = ((i - jb) * blk).astype(F32)
        scores(jb)
        alphas = []
        for h in range(N_HEADS):
            x = x_ref[h]
            c_j = offset * (_alibi_slope(h) * LOG2E)
            kept = keep_ref[pl.ds(h * nb + jb, 1), :] > 0.5
            m_old = m_ref[h:h + 1, :]
            m_new = jnp.maximum(m_old, jnp.where(kept, jnp.max(x, axis=0, keepdims=True) - c_j, NEG))
            alphas.append(jnp.exp2(m_old - m_new))
            p_ref[h] = jnp.exp2(x - jnp.where(kept, m_new + c_j, BIG)).astype(BF16)
            m_ref[h:h + 1, :] = m_new
        for h in range(N_HEADS):
            accl_ref[h] = alphas[h] * accl_ref[h] + _dot(values(h, jb), p_ref[h])
        return carry

    lax.fori_loop(0, i, past_block, 0)

    outs = [accl_ref[h, 0:HEAD_DIM, :] / accl_ref[h, HEAD_DIM:HEAD_DIM + 1, :] for h in range(N_HEADS)]
    o_ref[0] = jnp.concatenate(outs, axis=0).T.astype(BF16)


def _attn(q, kb, vt, km):
    b, s, _ = q.shape
    nq = s // MOBA_BLOCK
    return pl.pallas_call(
        _attn_kernel,
        grid=(b, nq),
        in_specs=[
            pl.BlockSpec((1, MOBA_BLOCK, ATTN_WIDTH), lambda bi, i: (bi, i, 0)),
            pl.BlockSpec((1, nq, MOBA_BLOCK, ATTN_WIDTH), lambda bi, i: (bi, 0, 0, 0)),
            pl.BlockSpec((1, nq, ATTN_WIDTH, MOBA_BLOCK), lambda bi, i: (bi, 0, 0, 0)),
            pl.BlockSpec((1, nq, ATTN_WIDTH), lambda bi, i: (bi, 0, 0)),
        ],
        out_specs=pl.BlockSpec((1, MOBA_BLOCK, ATTN_WIDTH), lambda bi, i: (bi, i, 0)),
        out_shape=jax.ShapeDtypeStruct((b, s, ATTN_WIDTH), BF16),
        scratch_shapes=[
            pltpu.VMEM((N_HEADS, LANES, MOBA_BLOCK), BF16),
            pltpu.VMEM((N_HEADS * nq, MOBA_BLOCK), F32),
            pltpu.VMEM((N_HEADS, MOBA_BLOCK, MOBA_BLOCK), F32),
            pltpu.VMEM((N_HEADS, MOBA_BLOCK), F32),
            pltpu.VMEM((N_HEADS, HEAD_DIM + BF16_ROWS, MOBA_BLOCK), F32),
            pltpu.VMEM((N_HEADS, MOBA_BLOCK, MOBA_BLOCK), F32),
            pltpu.VMEM((N_HEADS, MOBA_BLOCK, MOBA_BLOCK), BF16),
        ],
        compiler_params=_params(("arbitrary", "arbitrary")),
        name="moba_attn_prompt",
    )(q, kb, vt, km)


def _mix_head(attn_b, x, sh1, sc1, g1, wg_ref, wap_ref):
    hb = _mod_rms_norm(x, g1, sc1, sh1).astype(BF16)
    sga = _sigmoid(_dot(hb, wg_ref[:, :D_MODEL]))
    sgc = _sigmoid(_dot(hb, wg_ref[:, D_MODEL:]))
    return sga * _dot(attn_b, wap_ref[...]), sgc


def _mix_tail(cv, gated_attn, sgc, x, gt1, gln, bln, wcp_ref, wo_ref):
    mu = jnp.mean(cv, axis=-1, keepdims=True)
    cen = cv - mu
    var = jnp.mean(cen * cen, axis=-1, keepdims=True)
    y = cen * lax.rsqrt(var + EPS) * gln + bln
    cvp = _dot(_silu(y).astype(BF16), wcp_ref[...])
    mix = _dot((gated_attn + sgc * cvp).astype(BF16), wo_ref[...])
    return x + gt1 * mix


def _mix_prompt_kernel(x_ref, attn_ref, u_ref, halo_ref, sh1_ref, sc1_ref, gt1_ref, g1_ref, wdw_ref, bdw_ref,
                       gln_ref, bln_ref, wg_ref, wap_ref, wcp_ref, wo_ref, x1_ref, ext_ref, cv_ref):
    t = pl.program_id(1)
    tm = u_ref.shape[1]
    x = x_ref[0]
    gated_attn, sgc = _mix_head(attn_ref[0], x, sh1_ref[0], sc1_ref[0], g1_ref[...], wg_ref, wap_ref)
    ext_ref[0, 0:CONV_HALO, :] = jnp.where(t > 0, halo_ref[0], 0.0)
    ext_ref[0, CONV_HALO:, :] = u_ref[0]
    shifted_rows = tm + CONV_HALO - SUBLANES
    for r in range(1, SUBLANES):
        ext_ref[r, 0:shifted_rows, :] = ext_ref[0, r:r + shifted_rows, :]
    lead = CONV_HALO - (CONV_WIDTH - 1)
    for r0 in range(0, tm, CONV_ROWS):
        acc = jnp.zeros((CONV_ROWS // SUBLANES, SUBLANES, CONV_CH), F32)
        for w in range(CONV_WIDTH):
            r = (lead + w) % SUBLANES
            a = lead + w - r + r0
            acc = acc + ext_ref[r, a:a + CONV_ROWS, :].reshape(acc.shape) * wdw_ref[w]
        cv_ref[r0:r0 + CONV_ROWS, :] = acc.reshape(CONV_ROWS, CONV_CH) + bdw_ref[...]
    x1_ref[0] = _mix_tail(cv_ref[...], gated_attn, sgc, x, gt1_ref[0], gln_ref[...], bln_ref[...], wcp_ref, wo_ref)


def _mix_prompt(x3, attn, u, mod3, g_norm1, wdw, bdw, gln, bln, wg_b, wap_b, wcp_b, wo_b, *, tm):
    b, s, _ = x3.shape
    nt = s // tm
    halo_per_tile = tm // CONV_HALO
    tok = lambda width: pl.BlockSpec((1, tm, width), lambda bi, t: (bi, t, 0))
    full = lambda shape: pl.BlockSpec(shape, lambda bi, t: (0,) * len(shape))
    mod_spec = lambda chunk: pl.BlockSpec((1, 1, D_MODEL), lambda bi, t: (bi, 0, chunk))
    return pl.pallas_call(
        _mix_prompt_kernel,
        grid=(b, nt),
        in_specs=[
            tok(D_MODEL), tok(ATTN_WIDTH), tok(CONV_CH),
            pl.BlockSpec((1, CONV_HALO, CONV_CH), lambda bi, t: (bi, jnp.maximum(t * halo_per_tile - 1, 0), 0)),
            mod_spec(0), mod_spec(1), mod_spec(2),
            full((1, D_MODEL)),
            full(wdw.shape), full((1, CONV_CH)), full((1, CONV_CH)), full((1, CONV_CH)),
            full(wg_b.shape), full(wap_b.shape), full(wcp_b.shape), full(wo_b.shape),
        ],
        out_specs=tok(D_MODEL),
        out_shape=jax.ShapeDtypeStruct((b, s, D_MODEL), F32),
        scratch_shapes=[pltpu.VMEM((SUBLANES, CONV_HALO + tm, CONV_CH), F32), pltpu.VMEM((tm, CONV_CH), F32)],
        compiler_params=_params(("arbitrary", "arbitrary")),
        name="mix_prompt",
    )(x3, attn, u, u, mod3, mod3, mod3, g_norm1.reshape(1, D_MODEL), wdw, bdw, gln, bln, wg_b, wap_b, wcp_b, wo_b)


def _mix_sample_kernel(x_ref, attn_ref, u_ref, st_ref, sh1_ref, sc1_ref, gt1_ref, g1_ref, wdw_ref, bdw_ref,
                       gln_ref, bln_ref, wg_ref, wap_ref, wcp_ref, wo_ref, x1_ref, st_out_ref):
    u = u_ref[...]
    past = CONV_WIDTH - 1
    cv = u * wdw_ref[past:past + 1, :] + bdw_ref[...]
    for w in range(past):
        cv = cv + st_ref[w] * wdw_ref[w:w + 1, :]
    for w in range(1, past):
        st_out_ref[w - 1] = st_ref[w]
    st_out_ref[past - 1] = u
    x = x_ref[...]
    gated_attn, sgc = _mix_head(attn_ref[...].astype(BF16), x, sh1_ref[...], sc1_ref[...], g1_ref[...], wg_ref, wap_ref)
    x1_ref[...] = _mix_tail(cv, gated_attn, sgc, x, gt1_ref[...], gln_ref[...], bln_ref[...], wcp_ref, wo_ref)


def _mix_sample(x2, attn, u, st_t, mod_s, g_norm1, wdw, bdw, gln, bln, wg_b, wap_b, wcp_b, wo_b):
    n = x2.shape[0]
    full = lambda shape: pl.BlockSpec(shape, lambda j: (0,) * len(shape))
    mod_spec = lambda chunk: pl.BlockSpec((n, D_MODEL), lambda j: (0, chunk))
    return pl.pallas_call(
        _mix_sample_kernel,
        grid=(1,),
        in_specs=[
            full(x2.shape), full(attn.shape), full(u.shape), full(st_t.shape),
            mod_spec(0), mod_spec(1), mod_spec(2),
            full((1, D_MODEL)),
            full(wdw.shape), full((1, CONV_CH)), full((1, CONV_CH)), full((1, CONV_CH)),
            full(wg_b.shape), full(wap_b.shape), full(wcp_b.shape), full(wo_b.shape),
        ],
        out_specs=[full(x2.shape), full(st_t.shape)],
        out_shape=[jax.ShapeDtypeStruct(x2.shape, F32), jax.ShapeDtypeStruct(st_t.shape, F32)],
        compiler_params=_params(("arbitrary",)),
        name="mix_sample",
    )(x2, attn, u, st_t, mod_s, mod_s, mod_s, g_norm1.reshape(1, D_MODEL), wdw, bdw, gln, bln, wg_b, wap_b, wcp_b, wo_b)


def _ffn_kernel(*refs, host):
    if host is not None:
        refs = refs[1:]
        host_refs = refs[7:8 + HOST_PAGES]
        y_ref, gate_ref = refs[8 + HOST_PAGES:]
    else:
        y_ref = refs[7]
    x_ref, sh_ref, sc_ref, gt_ref, g2_ref, wi_ref, wo_ref = refs[:7]
    x = x_ref[0]
    hb = _mod_rms_norm(x, g2_ref[...], sc_ref[0], sh_ref[0]).astype(BF16)
    acc = jnp.zeros(x.shape, F32)
    for c in range(D_FF // FFN_CHUNK):
        c0 = c * FFN_CHUNK
        f_g = _dot(hb, wi_ref[:, c0:c0 + FFN_CHUNK])
        f_u = _dot(hb, wi_ref[:, D_FF + c0:D_FF + c0 + FFN_CHUNK])
        acc = acc + _dot((_silu(f_g) * f_u).astype(BF16), wo_ref[c0:c0 + FFN_CHUNK, :])
    y_ref[0] = x + gt_ref[0] * acc
    if host is not None:
        _host_gate(host_refs, gate_ref, host)


def _ffn(x3, mod3, g_norm2, wi_b, wo_b, *, tm, name, gate_host=None):
    g, s, _ = x3.shape
    r = mod3.shape[1]
    nt = s // tm
    host = _GateHost(*gate_host, grid=(g, nt)) if gate_host is not None else None

    def mod_spec(chunk):
        if r == 1:
            return pl.BlockSpec((1, 1, D_MODEL), lambda b, t, *_: (b, 0, chunk))
        return pl.BlockSpec((1, tm, D_MODEL), lambda b, t, *_: (b, t, chunk))

    tok = pl.BlockSpec((1, tm, D_MODEL), lambda b, t, *_: (b, t, 0))
    in_specs = [tok, mod_spec(3), mod_spec(4), mod_spec(5),
                _resident((1, D_MODEL)), _resident(wi_b.shape), _resident(wo_b.shape)]
    operands = [x3, mod3, mod3, mod3, g_norm2.reshape(1, D_MODEL), wi_b, wo_b]
    out_specs = [tok]
    out_shape = [jax.ShapeDtypeStruct(x3.shape, F32)]
    prefetch = []
    if host is not None:
        in_specs += host.in_specs()
        operands += host.operands()
        out_specs += [host.out_spec()]
        out_shape += [host.out_shape()]
        prefetch = [host.page_table]
    grid_spec = pltpu.PrefetchScalarGridSpec(
        num_scalar_prefetch=len(prefetch), grid=(g, nt), in_specs=in_specs, out_specs=out_specs)
    return pl.pallas_call(
        functools.partial(_ffn_kernel, host=host),
        grid_spec=grid_spec,
        out_shape=out_shape,
        compiler_params=_params(("arbitrary", "arbitrary")),
        name=name,
    )(*prefetch, *operands)


HOST_PAGES = 32
HOST_BLOCKS = HOST_PAGES // PAGES_PER_BLOCK


def _host_gate(refs, gate_ref, host):
    pages = refs[1:]
    qcol = jnp.broadcast_to(refs[0][0], (LANES, ATTN_WIDTH)).T
    chunk = host.chunk(pl.program_id(0), pl.program_id(1))
    sub = _iota((N_HEADS, LANES), 0)
    lane = _iota((N_HEADS, LANES), 1)
    gate = jnp.zeros((N_HEADS, LANES), F32)
    for blk in range(HOST_BLOCKS):
        kt = pages[blk * PAGES_PER_BLOCK][0]
        for pg in range(1, PAGES_PER_BLOCK):
            kt = kt + pages[blk * PAGES_PER_BLOCK + pg][0]
        prod = kt.reshape(ATTN_WIDTH, PAGE_SIZE) * qcol
        score = jnp.zeros((N_HEADS, LANES), F32)
        for h in range(N_HEADS):
            s_h = jnp.sum(prod[h * HEAD_DIM:(h + 1) * HEAD_DIM], axis=0, keepdims=True)
            score = jnp.where(sub == h, s_h, score)
        g_blk = jnp.sum(score, axis=1, keepdims=True) * (1.0 / MOBA_BLOCK)
        gate = jnp.where(lane == chunk * HOST_BLOCKS + blk, g_blk, gate)
    gate_ref[0, 0] = gate


class _GateHost:
    def __init__(self, page_table, q_s3, cache_kt, seq0, grid):
        n_pages = page_table.shape[1]
        self.chunks = n_pages // HOST_PAGES
        g, nt = grid
        self.nseq = g * nt // self.chunks
        assert g * nt == self.nseq * self.chunks and seq0 + self.nseq <= page_table.shape[0]
        self.page_table, self.q_s3, self.cache_kt, self.seq0, self.nt = page_table, q_s3, cache_kt, seq0, nt

    def _seq(self, b, t):
        return self.seq0 + (b * self.nt + t) // self.chunks

    def chunk(self, b, t):
        return (b * self.nt + t) % self.chunks

    def in_specs(self):
        def page_spec(r):
            return pl.BlockSpec(
                (1, N_HEADS, HEAD_DIM, PAGE_SIZE),
                lambda b, t, pt: (pt[self._seq(b, t), self.chunk(b, t) * HOST_PAGES + r], 0, 0, 0))
        return ([pl.BlockSpec((1, 1, ATTN_WIDTH), lambda b, t, pt: (self._seq(b, t), 0, 0))]
                + [page_spec(r) for r in range(HOST_PAGES)])

    def operands(self):
        return [self.q_s3] + [self.cache_kt] * HOST_PAGES

    def out_spec(self):
        return pl.BlockSpec((1, 1, N_HEADS, LANES), lambda b, t, pt: (self._seq(b, t) - self.seq0, self.chunk(b, t), 0, 0))

    def out_shape(self):
        return jax.ShapeDtypeStruct((self.nseq, self.chunks, N_HEADS, LANES), F32)


def _topk_kernel(gp_ref, idx_ref):
    chunks = gp_ref.shape[1]
    g = gp_ref[:, 0]
    for c in range(1, chunks):
        g = g + gp_ref[:, c]
    lane = _iota(g.shape, 2)
    g = jnp.where(lane < chunks * HOST_BLOCKS, g, -jnp.inf)
    out = jnp.zeros(g.shape, jnp.int32)
    for r in range(MOBA_TOPK):
        best = jnp.max(g, axis=2, keepdims=True)
        pick = jnp.min(jnp.where(g == best, lane, LANES), axis=2, keepdims=True)
        out = jnp.where(lane == r, pick, out)
        g = jnp.where(lane == pick, -jnp.inf, g)
    idx_ref[...] = out


def _topk(gate_parts):
    nseq = gate_parts.shape[0]
    out_shape = (nseq, N_HEADS, LANES)
    return pl.pallas_call(
        _topk_kernel,
        grid=(1,),
        in_specs=[pl.BlockSpec(gate_parts.shape, lambda j: (0, 0, 0, 0))],
        out_specs=pl.BlockSpec(out_shape, lambda j: (0, 0, 0)),
        out_shape=jax.ShapeDtypeStruct(out_shape, jnp.int32),
        compiler_params=_params(("arbitrary",)),
        name="moba_topk_sample",
    )(gate_parts)


SEL_PAGES = MOBA_TOPK * PAGES_PER_BLOCK
SATTN_HEADS_PER_STEP = 4


def _sattn_kernel(pt_ref, idx_ref, q_ref, kn_ref, vn_ref, *refs, past_len):
    del pt_ref
    n_pg = SATTN_HEADS_PER_STEP * SEL_PAGES
    k_refs = refs[:n_pg]
    v_refs = refs[n_pg:2 * n_pg]
    o_ref = refs[2 * n_pg]
    b = pl.program_id(0)
    hg = pl.program_id(1)
    lane = _iota((SUBLANES, PAGE_SIZE), 1)
    for hh in range(SATTN_HEADS_PER_STEP):
        h = hg * SATTN_HEADS_PER_STEP + hh
        q = q_ref[0, hh] * SM_SCALE
        q8 = jnp.broadcast_to(q, (SUBLANES, HEAD_DIM)).astype(BF16)
        k_cat = jnp.concatenate([k_refs[hh * SEL_PAGES + j][0, 0] for j in range(SEL_PAGES)], axis=1)
        v_cat = jnp.concatenate([v_refs[hh * SEL_PAGES + j][0, 0] for j in range(SEL_PAGES)], axis=1)
        pos = []
        for r in range(MOBA_TOPK):
            blk = idx_ref[(b * N_HEADS + h) * MOBA_TOPK + r]
            pos += [blk * MOBA_BLOCK + pg * PAGE_SIZE + lane for pg in range(PAGES_PER_BLOCK)]
        dist = (past_len - jnp.concatenate(pos, axis=1)).astype(F32)
        slope = lax.bitcast_convert_type(jnp.full(dist.shape, 126, jnp.int32) - h << 23, F32)
        s = _dot(q8, k_cat.astype(BF16)) - slope * dist
        s_own = jnp.sum(q * kn_ref[0, hh], axis=1, keepdims=True)
        m = jnp.maximum(jnp.max(s[0:1], axis=1, keepdims=True), s_own)
        p = jnp.exp(s - m)
        p_own = jnp.exp(s_own - m)
        l = jnp.sum(p[0:1], axis=1, keepdims=True) + p_own
        acc = _dot_nt(p.astype(BF16), v_cat.astype(BF16))[0:1] + p_own * vn_ref[0, hh]
        o_ref[0, hh] = acc / l


def _sattn(page_table, idx_flat, q4, kn4, vn4, cache_kt, cache_vt):
    nseq, n_pages = page_table.shape
    past_len = n_pages * PAGE_SIZE
    hps = SATTN_HEADS_PER_STEP

    def page_spec(hh, r, pg):
        def index_map(b, hg, pt, idx):
            h = hg * hps + hh
            return (pt[b, idx[(b * N_HEADS + h) * MOBA_TOPK + r] * PAGES_PER_BLOCK + pg], h, 0, 0)
        return pl.BlockSpec((1, 1, HEAD_DIM, PAGE_SIZE), index_map)

    tok = pl.BlockSpec((1, hps, 1, HEAD_DIM), lambda b, hg, pt, idx: (b, hg, 0, 0))
    pages = [page_spec(hh, r, pg) for hh in range(hps) for r in range(MOBA_TOPK) for pg in range(PAGES_PER_BLOCK)]
    grid_spec = pltpu.PrefetchScalarGridSpec(
        num_scalar_prefetch=2,
        grid=(nseq, N_HEADS // hps),
        in_specs=[tok, tok, tok] + pages + pages,
        out_specs=tok,
    )
    return pl.pallas_call(
        functools.partial(_sattn_kernel, past_len=past_len),
        grid_spec=grid_spec,
        out_shape=jax.ShapeDtypeStruct((nseq, N_HEADS, 1, HEAD_DIM), F32),
        compiler_params=_params(("arbitrary", "arbitrary")),
        name="moba_attn_sample",
    )(page_table, idx_flat, q4, kn4, vn4, *([cache_kt] * len(pages)), *([cache_vt] * len(pages)))


def kernel(x_prompt, x_sample, cache_k, cache_v, state_conv, page_table, c_prompt, c_sample, w_ada, b_ada, g_norm1, w_in, g_q, g_k, w_attn_proj, w_dwconv, b_dwconv, g_conv_ln, b_conv_ln, w_conv_proj, w_out, g_norm2, w_ffn_in, w_ffn_out):
    nb, seq, _ = x_prompt.shape
    ns = x_sample.shape[0]

    n_qkvu = 3 * ATTN_WIDTH + 2 * CONV_CH
    w_in_b = w_in[:, :n_qkvu].astype(BF16)
    wg_b = w_in[:, n_qkvu:].astype(BF16)
    wap_b = w_attn_proj.astype(BF16)
    wcp_b = w_conv_proj.astype(BF16)
    wo_b = w_out.astype(BF16)
    wfi_b = w_ffn_in.astype(BF16)
    wfo_b = w_ffn_out.astype(BF16)
    gq_t = jnp.tile(g_q, N_HEADS).reshape(1, ATTN_WIDTH)
    gk_t = jnp.tile(g_k, N_HEADS).reshape(1, ATTN_WIDTH)
    wdw = w_dwconv.reshape(CONV_WIDTH, CONV_CH)
    bdw = b_dwconv.reshape(1, CONV_CH)
    gln = g_conv_ln.reshape(1, CONV_CH)
    bln = b_conv_ln.reshape(1, CONV_CH)

    mod_p, mod_s = _ada(c_prompt, c_sample, w_ada, b_ada)
    mod_p3 = mod_p.reshape(nb, 1, 6 * D_MODEL)
    mod_s3 = mod_s.reshape(1, ns, 6 * D_MODEL)

    xs3 = x_sample.reshape(1, ns, D_MODEL)
    qs, ks, vs, us = _inproj(xs3, mod_s3, g_norm1, w_in_b, gq_t, gk_t, tm=ns, wq32=w_in[:, :ATTN_WIDTH])
    qs3 = qs.reshape(ns, 1, ATTN_WIDTH)
    cache_kt = jnp.transpose(cache_k, (0, 2, 3, 1))
    cache_vt = jnp.transpose(cache_v, (0, 2, 3, 1))

    q, k, v, kb, vt, u, km, gates_a = _inproj(x_prompt, mod_p3, g_norm1, w_in_b, gq_t, gk_t, tm=256,
                                              gate_host=(page_table, qs3, cache_kt, 0))
    attn = _attn(q, kb, vt, km.reshape(nb, seq // MOBA_BLOCK, ATTN_WIDTH))
    wdw8 = jnp.broadcast_to(wdw[:, None, :], (CONV_WIDTH, SUBLANES, CONV_CH))
    x1 = _mix_prompt(x_prompt, attn, u, mod_p3, g_norm1, wdw8, bdw, gln, bln, wg_b, wap_b, wcp_b, wo_b, tm=256)
    y_prompt, gates_b = _ffn(x1, mod_p3, g_norm2, wfi_b, wfo_b, tm=256, name="ffn_prompt",
                             gate_host=(page_table, qs3, cache_kt, gates_a.shape[0]))
    assert gates_a.shape[0] + gates_b.shape[0] == ns
    k_prompt = k.reshape(nb, seq, N_HEADS, HEAD_DIM)
    v_prompt = v.reshape(nb, seq, N_HEADS, HEAD_DIM)
    conv_prompt = u[:, seq - (CONV_WIDTH - 1):, :]

    idx = _topk(jnp.concatenate([gates_a, gates_b], axis=0))
    idx_flat = idx[:, :, :MOBA_TOPK].reshape(-1)
    head4 = lambda t: t.reshape(ns, N_HEADS, 1, HEAD_DIM)
    attn_s = _sattn(page_table, idx_flat, head4(qs), head4(ks), head4(vs), cache_kt, cache_vt)
    x1_s, st_new = _mix_sample(
        xs3[0], attn_s.reshape(ns, ATTN_WIDTH), us[0], jnp.transpose(state_conv, (1, 0, 2)),
        mod_s, g_norm1, wdw, bdw, gln, bln, wg_b, wap_b, wcp_b, wo_b)
    (y_sample,) = _ffn(x1_s.reshape(1, ns, D_MODEL), mod_s3, g_norm2, wfi_b, wfo_b, tm=ns, name="ffn_sample")

    return (y_prompt, y_sample.reshape(ns, 1, D_MODEL), k_prompt, v_prompt, conv_prompt,
            ks.reshape(ns, 1, N_HEADS, HEAD_DIM), vs.reshape(ns, 1, N_HEADS, HEAD_DIM),
            jnp.transpose(st_new, (1, 0, 2)))
```

```python
import functools

import jax
import jax.numpy as jnp
from jax import lax
from jax.experimental import pallas as pl
from jax.experimental.pallas import tpu as pltpu

D_MODEL = 1024
N_HEADS = 8
HEAD_DIM = 64
ATTN_WIDTH = N_HEADS * HEAD_DIM
CONV_CH = 512
CONV_WIDTH = 31
CONV_HALO = 32
CONV_ROWS = 32
MOBA_BLOCK = 256
MOBA_TOPK = 3
PAGE_SIZE = 128
PAGES_PER_BLOCK = MOBA_BLOCK // PAGE_SIZE
D_FF = 2816
FFN_CHUNK = 256
EPS = 1e-6
NEG = -1e30
BIG = 1e30
SM_SCALE = HEAD_DIM ** -0.5
LOG2E = 1.4426950408889634

LANES = 128
SUBLANES = 8
BF16_ROWS = 16
HEADS_PER_LANE_TILE = LANES // HEAD_DIM
VMEM_LIMIT = 56 * 1024 * 1024

F32 = jnp.float32
BF16 = jnp.bfloat16
HIGHEST = lax.Precision.HIGHEST


def _sigmoid(x):
    return 0.5 * jnp.tanh(0.5 * x) + 0.5


def _silu(x):
    return x * _sigmoid(x)


def _dot(a, b, precision=None):
    return jnp.dot(a, b, precision=precision, preferred_element_type=F32)


def _dot_nt(a, b):
    return lax.dot_general(a, b, (((1,), (1,)), ((), ())), preferred_element_type=F32)


def _iota(shape, dim):
    return lax.broadcasted_iota(jnp.int32, shape, dim)


def _alibi_slope(h):
    return 2.0 ** -(h + 1)


def _mod_rms_norm(x, gain, scale, shift):
    ms = jnp.mean(x * x, axis=-1, keepdims=True)
    return x * lax.rsqrt(ms + EPS) * gain * (1.0 + scale) + shift


def _resident(shape):
    return pl.BlockSpec(shape, lambda *_: (0,) * len(shape), pipeline_mode=pl.Buffered(1))


def _params(semantics):
    return pltpu.CompilerParams(dimension_semantics=semantics, vmem_limit_bytes=VMEM_LIMIT)


def _ada_kernel(cp_ref, cs_ref, w_ref, b_ref, mp_ref, ms_ref):
    w = w_ref[...]
    b = b_ref[...]
    mp_ref[...] = _dot(_silu(cp_ref[...]), w, HIGHEST) + b
    ms_ref[...] = _dot(_silu(cs_ref[...]), w, HIGHEST) + b


def _ada(c_prompt, c_sample, w_ada, b_ada):
    nb, ns = c_prompt.shape[0], c_sample.shape[0]
    n_out = w_ada.shape[1]
    tn = D_MODEL
    return pl.pallas_call(
        _ada_kernel,
        grid=(n_out // tn,),
        in_specs=[
            pl.BlockSpec((nb, D_MODEL), lambda j: (0, 0)),
            pl.BlockSpec((ns, D_MODEL), lambda j: (0, 0)),
            pl.BlockSpec((D_MODEL, tn), lambda j: (0, j)),
            pl.BlockSpec((1, tn), lambda j: (0, j)),
        ],
        out_specs=[
            pl.BlockSpec((nb, tn), lambda j: (0, j)),
            pl.BlockSpec((ns, tn), lambda j: (0, j)),
        ],
        out_shape=[jax.ShapeDtypeStruct((nb, n_out), F32), jax.ShapeDtypeStruct((ns, n_out), F32)],
        compiler_params=_params(("arbitrary",)),
        name="ada_mod",
    )(c_prompt, c_sample, w_ada, b_ada.reshape(1, n_out))


def _head_rms_norm(t, gain, group_ones):
    sq = t * t
    hi = sq.astype(BF16)
    lo = (sq - hi.astype(F32)).astype(BF16)
    ss = _dot(hi, group_ones) + _dot(lo, group_ones)
    return t * lax.rsqrt(ss * (1.0 / HEAD_DIM) + EPS) * gain


def _inproj_kernel(*refs, host):
    if host is not None:
        refs = refs[1:]
        x_ref, sh_ref, sc_ref, g1_ref, w_ref, gq_ref, gk_ref = refs[:7]
        host_refs = refs[7:8 + HOST_PAGES]
        q_ref, k_ref, v_ref, kb_ref, vt_ref, u_ref, km_ref, gate_ref = refs[8 + HOST_PAGES:]
    else:
        x_ref, sh_ref, sc_ref, g1_ref, w_ref, gq_ref, gk_ref, wq32_ref, q_ref, k_ref, v_ref, u_ref = refs
    prompt = host is not None
    h = _mod_rms_norm(x_ref[0], g1_ref[...], sc_ref[0], sh_ref[0])
    hb = h.astype(BF16)

    def proj(c0, c1):
        return _dot(hb, w_ref[:, c0:c1])

    a, c = ATTN_WIDTH, CONV_CH
    row = _iota((a, a), 0)
    col = _iota((a, a), 1)
    group_ones = jnp.where((row >> 6) == (col >> 6), 1.0, 0.0).astype(BF16)

    q_pre = proj(0, a) if prompt else _dot(h, wq32_ref[...], HIGHEST)
    q = _head_rms_norm(q_pre, gq_ref[...], group_ones)
    k = _head_rms_norm(proj(a, 2 * a), gk_ref[...], group_ones)
    v = proj(2 * a, 3 * a)
    q_ref[0] = q
    k_ref[0] = k
    v_ref[0] = v
    glu_a = proj(3 * a, 3 * a + c)
    glu_b = proj(3 * a + c, 3 * a + 2 * c)
    u_ref[0] = glu_a * _sigmoid(glu_b)
    if prompt:
        for r in range(k.shape[0] // MOBA_BLOCK):
            rows = slice(r * MOBA_BLOCK, (r + 1) * MOBA_BLOCK)
            kb_ref[0, r] = k[rows].astype(BF16)
            vt_ref[0, r] = v[rows].T.astype(BF16)
            km_ref[0, r] = jnp.mean(k[rows], axis=0, keepdims=True)
        _host_gate(host_refs, gate_ref, host)


def _inproj(x3, mod3, g_norm1, w_in_b, gq_t, gk_t, *, tm, wq32=None, gate_host=None):
    g, s, _ = x3.shape
    r = mod3.shape[1]
    nt = s // tm
    prompt = gate_host is not None
    host = _GateHost(*gate_host, grid=(g, nt)) if prompt else None

    def mod_spec(chunk):
        if r == 1:
            return pl.BlockSpec((1, 1, D_MODEL), lambda b, t, *_: (b, 0, chunk))
        return pl.BlockSpec((1, tm, D_MODEL), lambda b, t, *_: (b, t, chunk))

    tok = lambda width: pl.BlockSpec((1, tm, width), lambda b, t, *_: (b, t, 0))
    in_specs = [tok(D_MODEL), mod_spec(0), mod_spec(1),
                _resident((1, D_MODEL)), _resident(w_in_b.shape), _resident((1, ATTN_WIDTH)), _resident((1, ATTN_WIDTH))]
    operands = [x3, mod3, mod3, g_norm1.reshape(1, D_MODEL), w_in_b, gq_t, gk_t]
    out_specs = [tok(ATTN_WIDTH), tok(ATTN_WIDTH), tok(ATTN_WIDTH)]
    out_shape = [jax.ShapeDtypeStruct((g, s, ATTN_WIDTH), F32)] * 3
    if prompt:
        nblk = tm // MOBA_BLOCK
        blk_spec = lambda rows, cols: pl.BlockSpec((1, nblk, rows, cols), lambda b, t, *_: (b, t, 0, 0))
        in_specs += host.in_specs()
        operands += host.operands()
        out_specs += [blk_spec(MOBA_BLOCK, ATTN_WIDTH), blk_spec(ATTN_WIDTH, MOBA_BLOCK), tok(CONV_CH),
                      blk_spec(1, ATTN_WIDTH), host.out_spec()]
        out_shape += [jax.ShapeDtypeStruct((g, s // MOBA_BLOCK, MOBA_BLOCK, ATTN_WIDTH), BF16),
                      jax.ShapeDtypeStruct((g, s // MOBA_BLOCK, ATTN_WIDTH, MOBA_BLOCK), BF16),
                      jax.ShapeDtypeStruct((g, s, CONV_CH), F32),
                      jax.ShapeDtypeStruct((g, s // MOBA_BLOCK, 1, ATTN_WIDTH), F32),
                      host.out_shape()]
        prefetch = [host.page_table]
    else:
        in_specs += [pl.BlockSpec((D_MODEL, ATTN_WIDTH), lambda *_: (0, 0), pipeline_mode=pl.Buffered(1))]
        operands += [wq32]
        out_specs += [tok(CONV_CH)]
        out_shape += [jax.ShapeDtypeStruct((g, s, CONV_CH), F32)]
        prefetch = []
    grid_spec = pltpu.PrefetchScalarGridSpec(
        num_scalar_prefetch=len(prefetch), grid=(g, nt), in_specs=in_specs, out_specs=out_specs)
    return pl.pallas_call(
        functools.partial(_inproj_kernel, host=host),
        grid_spec=grid_spec,
        out_shape=out_shape,
        compiler_params=_params(("arbitrary", "arbitrary")),
        name="in_proj_prompt" if prompt else "in_proj_sample",
    )(*prefetch, *operands)


def _moba_keep_t(q_t, km, q_blk, keep_ref):
    nb = km.shape[0]
    assert nb == SUBLANES
    kmb = jnp.concatenate([km] * N_HEADS, axis=0)
    row = _iota(kmb.shape, 0)
    col = _iota(kmb.shape, 1)
    kmb = jnp.where((row >> 3) == (col >> 6), kmb, 0.0)
    gate_t = _dot(kmb, q_t, HIGHEST)
    n_idx = _iota((nb, q_t.shape[1]), 0)
    for h in range(N_HEADS):
        gate = gate_t[h * nb:(h + 1) * nb]
        beaten_by = jnp.zeros(gate.shape, jnp.int32)
        for s in range(1, nb):
            g_m = pltpu.roll(gate, s, axis=0)
            lower = n_idx >= s
            m_idx = jnp.where(lower, n_idx - s, n_idx - s + nb)
            beats = (g_m > gate) | (lower & (g_m == gate))
            beaten_by += jnp.where(beats & (m_idx < q_blk), 1, 0)
        keep = (n_idx < q_blk) & (beaten_by < MOBA_TOPK)
        keep_ref[h * nb:(h + 1) * nb, :] = jnp.where(keep, 1.0, 0.0)


def _attn_kernel(q_ref, kb_ref, vt_ref, km_ref, o_ref, qz_ref, keep_ref, bias_ref, m_ref, accl_ref, x_ref, p_ref):
    i = pl.program_id(1)
    nb = km_ref.shape[1]
    blk = MOBA_BLOCK
    q_t = q_ref[0].T
    _moba_keep_t(q_t, km_ref[0], i, keep_ref)

    key_r = _iota((blk, blk), 0)
    qry_c = _iota((blk, blk), 1)
    rel_t = (qry_c - key_r).astype(F32)
    causal = qry_c >= key_r
    qz_t = q_t * (SM_SCALE * LOG2E)
    pair_row = _iota((LANES, blk), 0)
    ones_rows = jnp.ones((BF16_ROWS, blk), BF16)
    for h in range(N_HEADS):
        hp, half = divmod(h, HEADS_PER_LANE_TILE)
        bias_ref[h] = rel_t * (-_alibi_slope(h) * LOG2E)
        qz_ref[h] = jnp.where((pair_row >> 6) == half, qz_t[hp * LANES:(hp + 1) * LANES], 0.0).astype(BF16)

    def scores(jb):
        for h in range(N_HEADS):
            hp = h // HEADS_PER_LANE_TILE
            x_ref[h] = _dot(kb_ref[0, jb, :, hp * LANES:(hp + 1) * LANES], qz_ref[h]) + bias_ref[h]

    def values(h, jb):
        return jnp.concatenate([vt_ref[0, jb, h * HEAD_DIM:(h + 1) * HEAD_DIM, :], ones_rows], axis=0)

    scores(i)
    for h in range(N_HEADS):
        x = jnp.where(causal, x_ref[h], NEG)
        m = jnp.max(x, axis=0, keepdims=True)
        p_ref[h] = jnp.exp2(x - m).astype(BF16)
        m_ref[h:h + 1, :] = m
    for h in range(N_HEADS):
        accl_ref[h] = _dot(values(h, i), p_ref[h])

    def past_block(jb, carry):
        offset = ((i - jb) * blk).astype(F32)
        scores(jb)
        alphas = []
        for h in range(N_HEADS):
            x = x_ref[h]
            c_j = offset * (_alibi_slope(h) * LOG2E)
            kept = keep_ref[pl.ds(h * nb + jb, 1), :] > 0.5
            m_old = m_ref[h:h + 1, :]
            m_new = jnp.maximum(m_old, jnp.where(kept, jnp.max(x, axis=0, keepdims=True) - c_j, NEG))
            alphas.append(jnp.exp2(m_old - m_new))
            p_ref[h] = jnp.exp2(x - jnp.where(kept, m_new + c_j, BIG)).astype(BF16)
            m_ref[h:h + 1, :] = m_new
        for h in range(N_HEADS):
            accl_ref[h] = alphas[h] * accl_ref[h] + _dot(values(h, jb), p_ref[h])
        return carry

    lax.fori_loop(0, i, past_block, 0)

    outs = [accl_ref[h, 0:HEAD_DIM, :] / accl_ref[h, HEAD_DIM:HEAD_DIM + 1, :] for h in range(N_HEADS)]
    o_ref[0] = jnp.concatenate(outs, axis=0).T.astype(BF16)


def _attn(q, kb, vt, km):
    b, s, _ = q.shape
    nq = s // MOBA_BLOCK
    return pl.pallas_call(
        _attn_kernel,
        grid=(b, nq),
        in_specs=[
            pl.BlockSpec((1, MOBA_BLOCK, ATTN_WIDTH), lambda bi, i: (bi, i, 0)),
            pl.BlockSpec((1, nq, MOBA_BLOCK, ATTN_WIDTH), lambda bi, i: (bi, 0, 0, 0)),
            pl.BlockSpec((1, nq, ATTN_WIDTH, MOBA_BLOCK), lambda bi, i: (bi, 0, 0, 0)),
            pl.BlockSpec((1, nq, ATTN_WIDTH), lambda bi, i: (bi, 0, 0)),
        ],
        out_specs=pl.BlockSpec((1, MOBA_BLOCK, ATTN_WIDTH), lambda bi, i: (bi, i, 0)),
        out_shape=jax.ShapeDtypeStruct((b, s, ATTN_WIDTH), BF16),
        scratch_shapes=[
            pltpu.VMEM((N_HEADS, LANES, MOBA_BLOCK), BF16),
            pltpu.VMEM((N_HEADS * nq, MOBA_BLOCK), F32),
            pltpu.VMEM((N_HEADS, MOBA_BLOCK, MOBA_BLOCK), F32),
            pltpu.VMEM((N_HEADS, MOBA_BLOCK), F32),
            pltpu.VMEM((N_HEADS, HEAD_DIM + BF16_ROWS, MOBA_BLOCK), F32),
            pltpu.VMEM((N_HEADS, MOBA_BLOCK, MOBA_BLOCK), F32),
            pltpu.VMEM((N_HEADS, MOBA_BLOCK, MOBA_BLOCK), BF16),
        ],
        compiler_params=_params(("arbitrary", "arbitrary")),
        name="moba_attn_prompt",
    )(q, kb, vt, km)


def _mix_head(attn_b, x, sh1, sc1, g1, wg_ref, wap_ref):
    hb = _mod_rms_norm(x, g1, sc1, sh1).astype(BF16)
    sga = _sigmoid(_dot(hb, wg_ref[:, :D_MODEL]))
    sgc = _sigmoid(_dot(hb, wg_ref[:, D_MODEL:]))
    return sga * _dot(attn_b, wap_ref[...]), sgc


def _mix_tail(cv, gated_attn, sgc, x, gt1, gln, bln, wcp_ref, wo_ref):
    mu = jnp.mean(cv, axis=-1, keepdims=True)
    cen = cv - mu
    var = jnp.mean(cen * cen, axis=-1, keepdims=True)
    y = cen * lax.rsqrt(var + EPS) * gln + bln
    cvp = _dot(_silu(y).astype(BF16), wcp_ref[...])
    mix = _dot((gated_attn + sgc * cvp).astype(BF16), wo_ref[...])
    return x + gt1 * mix


def _mix_prompt_kernel(x_ref, attn_ref, u_ref, halo_ref, sh1_ref, sc1_ref, gt1_ref, g1_ref, wdw_ref, bdw_ref,
                       gln_ref, bln_ref, wg_ref, wap_ref, wcp_ref, wo_ref, x1_ref, ext_ref, cv_ref):
    t = pl.program_id(1)
    tm = u_ref.shape[1]
    x = x_ref[0]
    gated_attn, sgc = _mix_head(attn_ref[0], x, sh1_ref[0], sc1_ref[0], g1_ref[...], wg_ref, wap_ref)
    ext_ref[0, 0:CONV_HALO, :] = jnp.where(t > 0, halo_ref[0], 0.0)
    ext_ref[0, CONV_HALO:, :] = u_ref[0]
    shifted_rows = tm + CONV_HALO - SUBLANES
    for r in range(1, SUBLANES):
        ext_ref[r, 0:shifted_rows, :] = ext_ref[0, r:r + shifted_rows, :]
    lead = CONV_HALO - (CONV_WIDTH - 1)
    for r0 in range(0, tm, CONV_ROWS):
        acc = jnp.zeros((CONV_ROWS // SUBLANES, SUBLANES, CONV_CH), F32)
        for w in range(CONV_WIDTH):
            r = (lead + w) % SUBLANES
            a = lead + w - r + r0
            acc = acc + ext_ref[r, a:a + CONV_ROWS, :].reshape(acc.shape) * wdw_ref[w]
        cv_ref[r0:r0 + CONV_ROWS, :] = acc.reshape(CONV_ROWS, CONV_CH) + bdw_ref[...]
    x1_ref[0] = _mix_tail(cv_ref[...], gated_attn, sgc, x, gt1_ref[0], gln_ref[...], bln_ref[...], wcp_ref, wo_ref)


def _mix_prompt(x3, attn, u, mod3, g_norm1, wdw, bdw, gln, bln, wg_b, wap_b, wcp_b, wo_b, *, tm):
    b, s, _ = x3.shape
    nt = s // tm
    halo_per_tile = tm // CONV_HALO
    tok = lambda width: pl.BlockSpec((1, tm, width), lambda bi, t: (bi, t, 0))
    full = lambda shape: pl.BlockSpec(shape, lambda bi, t: (0,) * len(shape))
    mod_spec = lambda chunk: pl.BlockSpec((1, 1, D_MODEL), lambda bi, t: (bi, 0, chunk))
    return pl.pallas_call(
        _mix_prompt_kernel,
        grid=(b, nt),
        in_specs=[
            tok(D_MODEL), tok(ATTN_WIDTH), tok(CONV_CH),
            pl.BlockSpec((1, CONV_HALO, CONV_CH), lambda bi, t: (bi, jnp.maximum(t * halo_per_tile - 1, 0), 0)),
            mod_spec(0), mod_spec(1), mod_spec(2),
            full((1, D_MODEL)),
            full(wdw.shape), full((1, CONV_CH)), full((1, CONV_CH)), full((1, CONV_CH)),
            full(wg_b.shape), full(wap_b.shape), full(wcp_b.shape), full(wo_b.shape),
        ],
        out_specs=tok(D_MODEL),
        out_shape=jax.ShapeDtypeStruct((b, s, D_MODEL), F32),
        scratch_shapes=[pltpu.VMEM((SUBLANES, CONV_HALO + tm, CONV_CH), F32), pltpu.VMEM((tm, CONV_CH), F32)],
        compiler_params=_params(("arbitrary", "arbitrary")),
        name="mix_prompt",
    )(x3, attn, u, u, mod3, mod3, mod3, g_norm1.reshape(1, D_MODEL), wdw, bdw, gln, bln, wg_b, wap_b, wcp_b, wo_b)


def _mix_sample_kernel(x_ref, attn_ref, u_ref, st_ref, sh1_ref, sc1_ref, gt1_ref, g1_ref, wdw_ref, bdw_ref,
                       gln_ref, bln_ref, wg_ref, wap_ref, wcp_ref, wo_ref, x1_ref, st_out_ref):
    u = u_ref[...]
    past = CONV_WIDTH - 1
    cv = u * wdw_ref[past:past + 1, :] + bdw_ref[...]
    for w in range(past):
        cv = cv + st_ref[w] * wdw_ref[w:w + 1, :]
    for w in range(1, past):
        st_out_ref[w - 1] = st_ref[w]
    st_out_ref[past - 1] = u
    x = x_ref[...]
    gated_attn, sgc = _mix_head(attn_ref[...].astype(BF16), x, sh1_ref[...], sc1_ref[...], g1_ref[...], wg_ref, wap_ref)
    x1_ref[...] = _mix_tail(cv, gated_attn, sgc, x, gt1_ref[...], gln_ref[...], bln_ref[...], wcp_ref, wo_ref)


def _mix_sample(x2, attn, u, st_t, mod_s, g_norm1, wdw, bdw, gln, bln, wg_b, wap_b, wcp_b, wo_b):
    n = x2.shape[0]
    full = lambda shape: pl.BlockSpec(shape, lambda j: (0,) * len(shape))
    mod_spec = lambda chunk: pl.BlockSpec((n, D_MODEL), lambda j: (0, chunk))
    return pl.pallas_call(
        _mix_sample_kernel,
        grid=(1,),
        in_specs=[
            full(x2.shape), full(attn.shape), full(u.shape), full(st_t.shape),
            mod_spec(0), mod_spec(1), mod_spec(2),
            full((1, D_MODEL)),
            full(wdw.shape), full((1, CONV_CH)), full((1, CONV_CH)), full((1, CONV_CH)),
            full(wg_b.shape), full(wap_b.shape), full(wcp_b.shape), full(wo_b.shape),
        ],
        out_specs=[full(x2.shape), full(st_t.shape)],
        out_shape=[jax.ShapeDtypeStruct(x2.shape, F32), jax.ShapeDtypeStruct(st_t.shape, F32)],
        compiler_params=_params(("arbitrary",)),
        name="mix_sample",
    )(x2, attn, u, st_t, mod_s, mod_s, mod_s, g_norm1.reshape(1, D_MODEL), wdw, bdw, gln, bln, wg_b, wap_b, wcp_b, wo_b)


def _ffn_kernel(*refs, host):
    if host is not None:
        refs = refs[1:]
        host_refs = refs[7:8 + HOST_PAGES]
        y_ref, gate_ref = refs[8 + HOST_PAGES:]
    else:
        y_ref = refs[7]
    x_ref, sh_ref, sc_ref, gt_ref, g2_ref, wi_ref, wo_ref = refs[:7]
    x = x_ref[0]
    hb = _mod_rms_norm(x, g2_ref[...], sc_ref[0], sh_ref[0]).astype(BF16)
    acc = jnp.zeros(x.shape, F32)
    for c in range(D_FF // FFN_CHUNK):
        c0 = c * FFN_CHUNK
        f_g = _dot(hb, wi_ref[:, c0:c0 + FFN_CHUNK])
        f_u = _dot(hb, wi_ref[:, D_FF + c0:D_FF + c0 + FFN_CHUNK])
        acc = acc + _dot((_silu(f_g) * f_u).astype(BF16), wo_ref[c0:c0 + FFN_CHUNK, :])
    y_ref[0] = x + gt_ref[0] * acc
    if host is not None:
        _host_gate(host_refs, gate_ref, host)


def _ffn(x3, mod3, g_norm2, wi_b, wo_b, *, tm, name, gate_host=None):
    g, s, _ = x3.shape
    r = mod3.shape[1]
    nt = s // tm
    host = _GateHost(*gate_host, grid=(g, nt)) if gate_host is not None else None

    def mod_spec(chunk):
        if r == 1:
            return pl.BlockSpec((1, 1, D_MODEL), lambda b, t, *_: (b, 0, chunk))
        return pl.BlockSpec((1, tm, D_MODEL), lambda b, t, *_: (b, t, chunk))

    tok = pl.BlockSpec((1, tm, D_MODEL), lambda b, t, *_: (b, t, 0))
    in_specs = [tok, mod_spec(3), mod_spec(4), mod_spec(5),
                _resident((1, D_MODEL)), _resident(wi_b.shape), _resident(wo_b.shape)]
    operands = [x3, mod3, mod3, mod3, g_norm2.reshape(1, D_MODEL), wi_b, wo_b]
    out_specs = [tok]
    out_shape = [jax.ShapeDtypeStruct(x3.shape, F32)]
    prefetch = []
    if host is not None:
        in_specs += host.in_specs()
        operands += host.operands()
        out_specs += [host.out_spec()]
        out_shape += [host.out_shape()]
        prefetch = [host.page_table]
    grid_spec = pltpu.PrefetchScalarGridSpec(
        num_scalar_prefetch=len(prefetch), grid=(g, nt), in_specs=in_specs, out_specs=out_specs)
    return pl.pallas_call(
        functools.partial(_ffn_kernel, host=host),
        grid_spec=grid_spec,
        out_shape=out_shape,
        compiler_params=_params(("arbitrary", "arbitrary")),
        name=name,
    )(*prefetch, *operands)


HOST_PAGES = 32
HOST_BLOCKS = HOST_PAGES // PAGES_PER_BLOCK


def _host_gate(refs, gate_ref, host):
    pages = refs[1:]
    qcol = jnp.broadcast_to(refs[0][0], (LANES, ATTN_WIDTH)).T
    chunk = host.chunk(pl.program_id(0), pl.program_id(1))
    sub = _iota((N_HEADS, LANES), 0)
    lane = _iota((N_HEADS, LANES), 1)
    gate = jnp.zeros((N_HEADS, LANES), F32)
    for h in range(N_HEADS):
        q_h = qcol[h * HEAD_DIM:(h + 1) * HEAD_DIM]
        col = jnp.zeros((SUBLANES, LANES), F32)
        for blk in range(HOST_BLOCKS):
            kt = pages[blk * PAGES_PER_BLOCK][0, h]
            for pg in range(1, PAGES_PER_BLOCK):
                kt = kt + pages[blk * PAGES_PER_BLOCK + pg][0, h]
            part = jnp.sum((kt * q_h).reshape(HEAD_DIM // SUBLANES, SUBLANES, PAGE_SIZE), axis=0)
            col = jnp.where(lane == chunk * HOST_BLOCKS + blk, jnp.sum(part, axis=1, keepdims=True), col)
        row = jnp.sum(col, axis=0, keepdims=True) * (1.0 / MOBA_BLOCK)
        gate = jnp.where(sub == h, row, gate)
    gate_ref[0, 0] = gate


class _GateHost:
    def __init__(self, page_table, q_s3, cache_kt, seq0, grid):
        n_pages = page_table.shape[1]
        self.chunks = n_pages // HOST_PAGES
        g, nt = grid
        self.nseq = g * nt // self.chunks
        assert g * nt == self.nseq * self.chunks and seq0 + self.nseq <= page_table.shape[0]
        self.page_table, self.q_s3, self.cache_kt, self.seq0, self.nt = page_table, q_s3, cache_kt, seq0, nt

    def _seq(self, b, t):
        return self.seq0 + (b * self.nt + t) // self.chunks

    def chunk(self, b, t):
        return (b * self.nt + t) % self.chunks

    def in_specs(self):
        def page_spec(r):
            return pl.BlockSpec(
                (1, N_HEADS, HEAD_DIM, PAGE_SIZE),
                lambda b, t, pt: (pt[self._seq(b, t), self.chunk(b, t) * HOST_PAGES + r], 0, 0, 0))
        return ([pl.BlockSpec((1, 1, ATTN_WIDTH), lambda b, t, pt: (self._seq(b, t), 0, 0))]
                + [page_spec(r) for r in range(HOST_PAGES)])

    def operands(self):
        return [self.q_s3] + [self.cache_kt] * HOST_PAGES

    def out_spec(self):
        return pl.BlockSpec((1, 1, N_HEADS, LANES), lambda b, t, pt: (self._seq(b, t) - self.seq0, self.chunk(b, t), 0, 0))

    def out_shape(self):
        return jax.ShapeDtypeStruct((self.nseq, self.chunks, N_HEADS, LANES), F32)


def _topk_kernel(gp_ref, idx_ref):
    chunks = gp_ref.shape[1]
    g = gp_ref[:, 0]
    for c in range(1, chunks):
        g = g + gp_ref[:, c]
    lane = _iota(g.shape, 2)
    g = jnp.where(lane < chunks * HOST_BLOCKS, g, -jnp.inf)
    out = jnp.zeros(g.shape, jnp.int32)
    for r in range(MOBA_TOPK):
        best = jnp.max(g, axis=2, keepdims=True)
        pick = jnp.min(jnp.where(g == best, lane, LANES), axis=2, keepdims=True)
        out = jnp.where(lane == r, pick, out)
        g = jnp.where(lane == pick, -jnp.inf, g)
    idx_ref[...] = out


def _topk(gate_parts):
    nseq = gate_parts.shape[0]
    out_shape = (nseq, N_HEADS, LANES)
    return pl.pallas_call(
        _topk_kernel,
        grid=(1,),
        in_specs=[pl.BlockSpec(gate_parts.shape, lambda j: (0, 0, 0, 0))],
        out_specs=pl.BlockSpec(out_shape, lambda j: (0, 0, 0)),
        out_shape=jax.ShapeDtypeStruct(out_shape, jnp.int32),
        compiler_params=_params(("arbitrary",)),
        name="moba_topk_sample",
    )(gate_parts)


SEL_PAGES = MOBA_TOPK * PAGES_PER_BLOCK
SATTN_HEADS_PER_STEP = 4


def _sattn_kernel(pt_ref, idx_ref, q_ref, kn_ref, vn_ref, *refs, past_len):
    del pt_ref
    n_pg = SATTN_HEADS_PER_STEP * SEL_PAGES
    k_refs = refs[:n_pg]
    v_refs = refs[n_pg:2 * n_pg]
    o_ref = refs[2 * n_pg]
    b = pl.program_id(0)
    hg = pl.program_id(1)
    lane = _iota((1, PAGE_SIZE), 1)
    for hh in range(SATTN_HEADS_PER_STEP):
        h = hg * SATTN_HEADS_PER_STEP + hh
        q = q_ref[0, hh] * SM_SCALE
        k_cat = jnp.concatenate([k_refs[hh * SEL_PAGES + j][0, 0] for j in range(SEL_PAGES)], axis=1)
        v_cat = jnp.concatenate([v_refs[hh * SEL_PAGES + j][0, 0] for j in range(SEL_PAGES)], axis=1)
        pos = []
        for r in range(MOBA_TOPK):
            blk = idx_ref[(b * N_HEADS + h) * MOBA_TOPK + r]
            pos += [blk * MOBA_BLOCK + pg * PAGE_SIZE + lane for pg in range(PAGES_PER_BLOCK)]
        dist = (past_len - jnp.concatenate(pos, axis=1)).astype(F32)
        slope = lax.bitcast_convert_type(jnp.full(dist.shape, 126, jnp.int32) - h << 23, F32)
        s = jnp.sum(k_cat * q, axis=0, keepdims=True) - slope * dist
        s_own = jnp.sum(q * kn_ref[0, hh], axis=0, keepdims=True)
        m = jnp.maximum(jnp.max(s, axis=1, keepdims=True), s_own)
        p = jnp.exp(s - m)
        p_own = jnp.exp(s_own - m)
        l = jnp.sum(p, axis=1, keepdims=True) + p_own
        acc = jnp.sum(v_cat * p, axis=1, keepdims=True) + p_own * vn_ref[0, hh]
        o_ref[0, hh] = acc / l


def _sattn(page_table, idx_flat, q4, kn4, vn4, cache_kt, cache_vt):
    nseq, n_pages = page_table.shape
    past_len = n_pages * PAGE_SIZE
    hps = SATTN_HEADS_PER_STEP

    def page_spec(hh, r, pg):
        def index_map(b, hg, pt, idx):
            h = hg * hps + hh
            return (pt[b, idx[(b * N_HEADS + h) * MOBA_TOPK + r] * PAGES_PER_BLOCK + pg], h, 0, 0)
        return pl.BlockSpec((1, 1, HEAD_DIM, PAGE_SIZE), index_map)

    tok = pl.BlockSpec((1, hps, HEAD_DIM, 1), lambda b, hg, pt, idx: (b, hg, 0, 0))
    pages = [page_spec(hh, r, pg) for hh in range(hps) for r in range(MOBA_TOPK) for pg in range(PAGES_PER_BLOCK)]
    grid_spec = pltpu.PrefetchScalarGridSpec(
        num_scalar_prefetch=2,
        grid=(nseq, N_HEADS // hps),
        in_specs=[tok, tok, tok] + pages + pages,
        out_specs=tok,
    )
    return pl.pallas_call(
        functools.partial(_sattn_kernel, past_len=past_len),
        grid_spec=grid_spec,
        out_shape=jax.ShapeDtypeStruct((nseq, N_HEADS, HEAD_DIM, 1), F32),
        compiler_params=_params(("arbitrary", "arbitrary")),
        name="moba_attn_sample",
    )(page_table, idx_flat, q4, kn4, vn4, *([cache_kt] * len(pages)), *([cache_vt] * len(pages)))


def kernel(x_prompt, x_sample, cache_k, cache_v, state_conv, page_table, c_prompt, c_sample, w_ada, b_ada, g_norm1, w_in, g_q, g_k, w_attn_proj, w_dwconv, b_dwconv, g_conv_ln, b_conv_ln, w_conv_proj, w_out, g_norm2, w_ffn_in, w_ffn_out):
    nb, seq, _ = x_prompt.shape
    ns = x_sample.shape[0]

    n_qkvu = 3 * ATTN_WIDTH + 2 * CONV_CH
    w_in_b = w_in[:, :n_qkvu].astype(BF16)
    wg_b = w_in[:, n_qkvu:].astype(BF16)
    wap_b = w_attn_proj.astype(BF16)
    wcp_b = w_conv_proj.astype(BF16)
    wo_b = w_out.astype(BF16)
    wfi_b = w_ffn_in.astype(BF16)
    wfo_b = w_ffn_out.astype(BF16)
    gq_t = jnp.tile(g_q, N_HEADS).reshape(1, ATTN_WIDTH)
    gk_t = jnp.tile(g_k, N_HEADS).reshape(1, ATTN_WIDTH)
    wdw = w_dwconv.reshape(CONV_WIDTH, CONV_CH)
    bdw = b_dwconv.reshape(1, CONV_CH)
    gln = g_conv_ln.reshape(1, CONV_CH)
    bln = b_conv_ln.reshape(1, CONV_CH)

    mod_p, mod_s = _ada(c_prompt, c_sample, w_ada, b_ada)
    mod_p3 = mod_p.reshape(nb, 1, 6 * D_MODEL)
    mod_s3 = mod_s.reshape(1, ns, 6 * D_MODEL)

    xs3 = x_sample.reshape(1, ns, D_MODEL)
    qs, ks, vs, us = _inproj(xs3, mod_s3, g_norm1, w_in_b, gq_t, gk_t, tm=ns, wq32=w_in)
    qs3 = qs.reshape(ns, 1, ATTN_WIDTH)
    cache_kt = jnp.transpose(cache_k, (0, 2, 3, 1))
    cache_vt = jnp.transpose(cache_v, (0, 2, 3, 1))

    q, k, v, kb, vt, u, km, gates_a = _inproj(x_prompt, mod_p3, g_norm1, w_in_b, gq_t, gk_t, tm=256,
                                              gate_host=(page_table, qs3, cache_kt, 0))
    attn = _attn(q, kb, vt, km.reshape(nb, seq // MOBA_BLOCK, ATTN_WIDTH))
    wdw8 = jnp.broadcast_to(wdw[:, None, :], (CONV_WIDTH, SUBLANES, CONV_CH))
    x1 = _mix_prompt(x_prompt, attn, u, mod_p3, g_norm1, wdw8, bdw, gln, bln, wg_b, wap_b, wcp_b, wo_b, tm=256)
    y_prompt, gates_b = _ffn(x1, mod_p3, g_norm2, wfi_b, wfo_b, tm=256, name="ffn_prompt",
                             gate_host=(page_table, qs3, cache_kt, gates_a.shape[0]))
    assert gates_a.shape[0] + gates_b.shape[0] == ns
    k_prompt = k.reshape(nb, seq, N_HEADS, HEAD_DIM)
    v_prompt = v.reshape(nb, seq, N_HEADS, HEAD_DIM)
    conv_prompt = u[:, seq - (CONV_WIDTH - 1):, :]

    idx = _topk(jnp.concatenate([gates_a, gates_b], axis=0))
    idx_flat = idx[:, :, :MOBA_TOPK].reshape(-1)
    head4 = lambda t: t.reshape(ns, N_HEADS, HEAD_DIM, 1)
    attn_s = _sattn(page_table, idx_flat, head4(qs), head4(ks), head4(vs), cache_kt, cache_vt)
    x1_s, st_new = _mix_sample(
        xs3[0], attn_s.reshape(ns, ATTN_WIDTH), us[0], jnp.transpose(state_conv, (1, 0, 2)),
        mod_s, g_norm1, wdw, bdw, gln, bln, wg_b, wap_b, wcp_b, wo_b)
    (y_sample,) = _ffn(x1_s.reshape(1, ns, D_MODEL), mod_s3, g_norm2, wfi_b, wfo_b, tm=ns, name="ffn_sample")

    return (y_prompt, y_sample.reshape(ns, 1, D_MODEL), k_prompt, v_prompt, conv_prompt,
            ks.reshape(ns, 1, N_HEADS, HEAD_DIM), vs.reshape(ns, 1, N_HEADS, HEAD_DIM),
            jnp.transpose(st_new, (1, 0, 2)))
```

```python
import functools

import jax
import jax.numpy as jnp
from jax import lax
from jax.experimental import pallas as pl
from jax.experimental.pallas import tpu as pltpu

D_MODEL = 1024
N_HEADS = 8
HEAD_DIM = 64
ATTN_WIDTH = N_HEADS * HEAD_DIM
CONV_CH = 512
CONV_WIDTH = 31
CONV_HALO = 32
CONV_ROWS = 32
MOBA_BLOCK = 256
MOBA_TOPK = 3
PAGE_SIZE = 128
PAGES_PER_BLOCK = MOBA_BLOCK // PAGE_SIZE
N_QKVU = 3 * ATTN_WIDTH + 2 * CONV_CH
GATE_COLS = 512
D_FF = 2816
FFN_CHUNK = 256
EPS = 1e-6
NEG = -1e30
BIG = 1e30
SM_SCALE = HEAD_DIM ** -0.5
LOG2E = 1.4426950408889634

LANES = 128
SUBLANES = 8
BF16_ROWS = 16
HEADS_PER_LANE_TILE = LANES // HEAD_DIM
VMEM_LIMIT = 56 * 1024 * 1024

F32 = jnp.float32
BF16 = jnp.bfloat16
HIGHEST = lax.Precision.HIGHEST


def _sigmoid(x):
    return 0.5 * jnp.tanh(0.5 * x) + 0.5


def _silu(x):
    return x * _sigmoid(x)


def _dot(a, b, precision=None):
    return jnp.dot(a, b, precision=precision, preferred_element_type=F32)


def _dot_nt(a, b):
    return lax.dot_general(a, b, (((1,), (1,)), ((), ())), preferred_element_type=F32)


def _iota(shape, dim):
    return lax.broadcasted_iota(jnp.int32, shape, dim)


def _alibi_slope(h):
    return 2.0 ** -(h + 1)


def _mod_rms_norm(x, gain, scale, shift):
    ms = jnp.mean(x * x, axis=-1, keepdims=True)
    return x * lax.rsqrt(ms + EPS) * gain * (1.0 + scale) + shift


def _resident(shape):
    return pl.BlockSpec(shape, lambda *_: (0,) * len(shape), pipeline_mode=pl.Buffered(1))


def _weight_cols(width, col_block):
    return pl.BlockSpec((D_MODEL, width), lambda *_: (0, col_block), pipeline_mode=pl.Buffered(1))


def _params(semantics):
    return pltpu.CompilerParams(dimension_semantics=semantics, vmem_limit_bytes=VMEM_LIMIT)


def _ada_kernel(cp_ref, cs_ref, w_ref, b_ref, mp_ref, ms_ref):
    w = w_ref[...]
    b = b_ref[...]
    mp_ref[...] = _dot(_silu(cp_ref[...]), w, HIGHEST) + b
    ms_ref[...] = _dot(_silu(cs_ref[...]), w, HIGHEST) + b


def _ada(c_prompt, c_sample, w_ada, b_ada):
    nb, ns = c_prompt.shape[0], c_sample.shape[0]
    n_out = w_ada.shape[1]
    tn = D_MODEL
    return pl.pallas_call(
        _ada_kernel,
        grid=(n_out // tn,),
        in_specs=[
            pl.BlockSpec((nb, D_MODEL), lambda j: (0, 0)),
            pl.BlockSpec((ns, D_MODEL), lambda j: (0, 0)),
            pl.BlockSpec((D_MODEL, tn), lambda j: (0, j)),
            pl.BlockSpec((1, tn), lambda j: (0, j)),
        ],
        out_specs=[
            pl.BlockSpec((nb, tn), lambda j: (0, j)),
            pl.BlockSpec((ns, tn), lambda j: (0, j)),
        ],
        out_shape=[jax.ShapeDtypeStruct((nb, n_out), F32), jax.ShapeDtypeStruct((ns, n_out), F32)],
        compiler_params=_params(("arbitrary",)),
        name="ada_mod",
    )(c_prompt, c_sample, w_ada, b_ada.reshape(1, n_out))


def _head_rms_norm(t, gain, group_ones):
    sq = t * t
    hi = sq.astype(BF16)
    lo = (sq - hi.astype(F32)).astype(BF16)
    ss = _dot(hi, group_ones) + _dot(lo, group_ones)
    return t * lax.rsqrt(ss * (1.0 / HEAD_DIM) + EPS) * gain


def _inproj_kernel(*refs, host):
    if host is not None:
        pt_ref, x_ref, sh_ref, sc_ref, g1_ref, w_ref, gq_ref, gk_ref, qs_ref, kt_hbm = refs[:10]
        q_ref, k_ref, v_ref, kb_ref, vt_ref, u_ref, km_ref, gate_ref, page_buf, page_sem = refs[10:]
        slot = host.fetch(pt_ref, kt_hbm, page_buf, page_sem)
    else:
        x_ref, sh_ref, sc_ref, g1_ref, w_ref, gq_ref, gk_ref, wq32_ref, q_ref, k_ref, v_ref, u_ref = refs
    prompt = host is not None
    h = _mod_rms_norm(x_ref[0], g1_ref[...], sc_ref[0], sh_ref[0])
    hb = h.astype(BF16)

    def proj(c0, c1):
        return _dot(hb, w_ref[:, c0:c1])

    a, c = ATTN_WIDTH, CONV_CH
    row = _iota((a, a), 0)
    col = _iota((a, a), 1)
    group_ones = jnp.where((row >> 6) == (col >> 6), 1.0, 0.0).astype(BF16)

    q_pre = proj(0, a) if prompt else _dot(h, wq32_ref[...], HIGHEST)
    q = _head_rms_norm(q_pre, gq_ref[...], group_ones)
    k = _head_rms_norm(proj(a, 2 * a), gk_ref[...], group_ones)
    v = proj(2 * a, 3 * a)
    q_ref[0] = q
    k_ref[0] = k
    v_ref[0] = v
    glu_a = proj(3 * a, 3 * a + c)
    glu_b = proj(3 * a + c, 3 * a + 2 * c)
    u_ref[0] = glu_a * _sigmoid(glu_b)
    if prompt:
        for r in range(k.shape[0] // MOBA_BLOCK):
            rows = slice(r * MOBA_BLOCK, (r + 1) * MOBA_BLOCK)
            kb_ref[0, r] = k[rows].astype(BF16)
            vt_ref[0, r] = v[rows].T.astype(BF16)
            km_ref[0, r] = jnp.mean(k[rows], axis=0, keepdims=True)
        _host_gate(qs_ref, page_buf.at[slot], gate_ref, host)


def _inproj(x3, mod3, g_norm1, w_in_b, gq_t, gk_t, *, tm, wq32=None, gate_host=None):
    g, s, _ = x3.shape
    r = mod3.shape[1]
    nt = s // tm
    prompt = gate_host is not None
    host = _GateHost(*gate_host, grid=(g, nt)) if prompt else None

    def mod_spec(chunk):
        if r == 1:
            return pl.BlockSpec((1, 1, D_MODEL), lambda b, t, *_: (b, 0, chunk))
        return pl.BlockSpec((1, tm, D_MODEL), lambda b, t, *_: (b, t, chunk))

    tok = lambda width: pl.BlockSpec((1, tm, width), lambda b, t, *_: (b, t, 0))
    in_specs = [tok(D_MODEL), mod_spec(0), mod_spec(1),
                _resident((1, D_MODEL)), _weight_cols(N_QKVU, 0), _resident((1, ATTN_WIDTH)), _resident((1, ATTN_WIDTH))]
    operands = [x3, mod3, mod3, g_norm1.reshape(1, D_MODEL), w_in_b, gq_t, gk_t]
    out_specs = [tok(ATTN_WIDTH), tok(ATTN_WIDTH), tok(ATTN_WIDTH)]
    out_shape = [jax.ShapeDtypeStruct((g, s, ATTN_WIDTH), F32)] * 3
    if prompt:
        nblk = tm // MOBA_BLOCK
        blk_spec = lambda rows, cols: pl.BlockSpec((1, nblk, rows, cols), lambda b, t, *_: (b, t, 0, 0))
        in_specs += host.in_specs()
        operands += host.operands()
        out_specs += [blk_spec(MOBA_BLOCK, ATTN_WIDTH), blk_spec(ATTN_WIDTH, MOBA_BLOCK), tok(CONV_CH),
                      blk_spec(1, ATTN_WIDTH), host.out_spec()]
        out_shape += [jax.ShapeDtypeStruct((g, s // MOBA_BLOCK, MOBA_BLOCK, ATTN_WIDTH), BF16),
                      jax.ShapeDtypeStruct((g, s // MOBA_BLOCK, ATTN_WIDTH, MOBA_BLOCK), BF16),
                      jax.ShapeDtypeStruct((g, s, CONV_CH), F32),
                      jax.ShapeDtypeStruct((g, s // MOBA_BLOCK, 1, ATTN_WIDTH), F32),
                      host.out_shape()]
        prefetch = [host.page_table]
    else:
        in_specs += [_weight_cols(ATTN_WIDTH, 0)]
        operands += [wq32]
        out_specs += [tok(CONV_CH)]
        out_shape += [jax.ShapeDtypeStruct((g, s, CONV_CH), F32)]
        prefetch = []
    grid_spec = pltpu.PrefetchScalarGridSpec(
        num_scalar_prefetch=len(prefetch), grid=(g, nt), in_specs=in_specs, out_specs=out_specs,
        scratch_shapes=host.scratch_shapes() if prompt else [])
    return pl.pallas_call(
        functools.partial(_inproj_kernel, host=host),
        grid_spec=grid_spec,
        out_shape=out_shape,
        compiler_params=_params(("arbitrary", "arbitrary")),
        name="in_proj_prompt" if prompt else "in_proj_sample",
    )(*prefetch, *operands)


def _moba_keep_t(q_t, km, q_blk, keep_ref):
    nb = km.shape[0]
    assert nb == SUBLANES
    kmb = jnp.concatenate([km] * N_HEADS, axis=0)
    row = _iota(kmb.shape, 0)
    col = _iota(kmb.shape, 1)
    kmb = jnp.where((row >> 3) == (col >> 6), kmb, 0.0)
    gate_t = _dot(kmb, q_t, HIGHEST)
    n_idx = _iota((nb, q_t.shape[1]), 0)
    for h in range(N_HEADS):
        gate = gate_t[h * nb:(h + 1) * nb]
        beaten_by = jnp.zeros(gate.shape, jnp.int32)
        for s in range(1, nb):
            g_m = pltpu.roll(gate, s, axis=0)
            lower = n_idx >= s
            m_idx = jnp.where(lower, n_idx - s, n_idx - s + nb)
            beats = (g_m > gate) | (lower & (g_m == gate))
            beaten_by += jnp.where(beats & (m_idx < q_blk), 1, 0)
        keep = (n_idx < q_blk) & (beaten_by < MOBA_TOPK)
        keep_ref[h * nb:(h + 1) * nb, :] = jnp.where(keep, 1.0, 0.0)


def _attn_kernel(q_ref, kb_ref, vt_ref, km_ref, o_ref, qz_ref, keep_ref, bias_ref, m_ref, accl_ref, x_ref, p_ref):
    i = pl.program_id(1)
    nb = km_ref.shape[1]
    blk = MOBA_BLOCK
    q_t = q_ref[0].T
    _moba_keep_t(q_t, km_ref[0], i, keep_ref)

    key_r = _iota((blk, blk), 0)
    qry_c = _iota((blk, blk), 1)
    rel_t = (qry_c - key_r).astype(F32)
    causal = qry_c >= key_r
    qz_t = q_t * (SM_SCALE * LOG2E)
    pair_row = _iota((LANES, blk), 0)
    ones_rows = jnp.ones((BF16_ROWS, blk), BF16)
    for h in range(N_HEADS):
        hp, half = divmod(h, HEADS_PER_LANE_TILE)
        bias_ref[h] = rel_t * (-_alibi_slope(h) * LOG2E)
        qz_ref[h] = jnp.where((pair_row >> 6) == half, qz_t[hp * LANES:(hp + 1) * LANES], 0.0).astype(BF16)

    def scores(jb):
        for h in range(N_HEADS):
            hp = h // HEADS_PER_LANE_TILE
            x_ref[h] = _dot(kb_ref[0, jb, :, hp * LANES:(hp + 1) * LANES], qz_ref[h]) + bias_ref[h]

    def values(h, jb):
        return jnp.concatenate([vt_ref[0, jb, h * HEAD_DIM:(h + 1) * HEAD_DIM, :], ones_rows], axis=0)

    scores(i)
    for h in range(N_HEADS):
        x = jnp.where(causal, x_ref[h], NEG)
        m = jnp.max(x, axis=0, keepdims=True)
        p_ref[h] = jnp.exp2(x - m).astype(BF16)
        m_ref[h:h + 1, :] = m
    for h in range(N_HEADS):
        accl_ref[h] = _dot(values(h, i), p_ref[h])

    def past_block(jb, carry):
        offset = ((i - jb) * blk).astype(F32)
        scores(jb)
        alphas = []
        for h in range(N_HEADS):
            x = x_ref[h]
            c_j = offset * (_alibi_slope(h) * LOG2E)
            kept = keep_ref[pl.ds(h * nb + jb, 1), :] > 0.5
            m_old = m_ref[h:h + 1, :]
            m_new = jnp.maximum(m_old, jnp.where(kept, jnp.max(x, axis=0, keepdims=True) - c_j, NEG))
            alphas.append(jnp.exp2(m_old - m_new))
            p_ref[h] = jnp.exp2(x - jnp.where(kept, m_new + c_j, BIG)).astype(BF16)
            m_ref[h:h + 1, :] = m_new
        for h in range(N_HEADS):
            accl_ref[h] = alphas[h] * accl_ref[h] + _dot(values(h, jb), p_ref[h])
        return carry

    lax.fori_loop(0, i, past_block, 0)

    outs = [accl_ref[h, 0:HEAD_DIM, :] / accl_ref[h, HEAD_DIM:HEAD_DIM + 1, :] for h in range(N_HEADS)]
    o_ref[0] = jnp.concatenate(outs, axis=0).T.astype(BF16)


def _attn(q, kb, vt, km):
    b, s, _ = q.shape
    nq = s // MOBA_BLOCK
    return pl.pallas_call(
        _attn_kernel,
        grid=(b, nq),
        in_specs=[
            pl.BlockSpec((1, MOBA_BLOCK, ATTN_WIDTH), lambda bi, i: (bi, i, 0)),
            pl.BlockSpec((1, nq, MOBA_BLOCK, ATTN_WIDTH), lambda bi, i: (bi, 0, 0, 0)),
            pl.BlockSpec((1, nq, ATTN_WIDTH, MOBA_BLOCK), lambda bi, i: (bi, 0, 0, 0)),
            pl.BlockSpec((1, nq, ATTN_WIDTH), lambda bi, i: (bi, 0, 0)),
        ],
        out_specs=pl.BlockSpec((1, MOBA_BLOCK, ATTN_WIDTH), lambda bi, i: (bi, i, 0)),
        out_shape=jax.ShapeDtypeStruct((b, s, ATTN_WIDTH), BF16),
        scratch_shapes=[
            pltpu.VMEM((N_HEADS, LANES, MOBA_BLOCK), BF16),
            pltpu.VMEM((N_HEADS * nq, MOBA_BLOCK), F32),
            pltpu.VMEM((N_HEADS, MOBA_BLOCK, MOBA_BLOCK), F32),
            pltpu.VMEM((N_HEADS, MOBA_BLOCK), F32),
            pltpu.VMEM((N_HEADS, HEAD_DIM + BF16_ROWS, MOBA_BLOCK), F32),
            pltpu.VMEM((N_HEADS, MOBA_BLOCK, MOBA_BLOCK), F32),
            pltpu.VMEM((N_HEADS, MOBA_BLOCK, MOBA_BLOCK), BF16),
        ],
        compiler_params=_params(("arbitrary", "arbitrary")),
        name="moba_attn_prompt",
    )(q, kb, vt, km)


def _gate_weight_specs():
    first = N_QKVU // GATE_COLS
    return [_weight_cols(GATE_COLS, first + j) for j in range(2 * D_MODEL // GATE_COLS)]


def _mix_head(attn_b, x, sh1, sc1, g1, wg_refs, wap_ref):
    hb = _mod_rms_norm(x, g1, sc1, sh1).astype(BF16)
    gates = [_dot(hb, w_ref[...]) for w_ref in wg_refs]
    per_gate = len(gates) // 2
    sga = _sigmoid(jnp.concatenate(gates[:per_gate], axis=1))
    sgc = _sigmoid(jnp.concatenate(gates[per_gate:], axis=1))
    return sga * _dot(attn_b, wap_ref[...]), sgc


def _mix_tail(cv, gated_attn, sgc, x, gt1, gln, bln, wcp_ref, wo_ref):
    mu = jnp.mean(cv, axis=-1, keepdims=True)
    cen = cv - mu
    var = jnp.mean(cen * cen, axis=-1, keepdims=True)
    y = cen * lax.rsqrt(var + EPS) * gln + bln
    cvp = _dot(_silu(y).astype(BF16), wcp_ref[...])
    mix = _dot((gated_attn + sgc * cvp).astype(BF16), wo_ref[...])
    return x + gt1 * mix


def _mix_prompt_kernel(x_ref, attn_ref, u_ref, halo_ref, sh1_ref, sc1_ref, gt1_ref, g1_ref, wdw_ref, bdw_ref,
                       gln_ref, bln_ref, wg0_ref, wg1_ref, wg2_ref, wg3_ref, wap_ref, wcp_ref, wo_ref,
                       x1_ref, ext_ref, cv_ref):
    t = pl.program_id(1)
    tm = u_ref.shape[1]
    x = x_ref[0]
    gated_attn, sgc = _mix_head(attn_ref[0], x, sh1_ref[0], sc1_ref[0], g1_ref[...],
                                (wg0_ref, wg1_ref, wg2_ref, wg3_ref), wap_ref)
    ext_ref[0, 0:CONV_HALO, :] = jnp.where(t > 0, halo_ref[0], 0.0)
    ext_ref[0, CONV_HALO:, :] = u_ref[0]
    shifted_rows = tm + CONV_HALO - SUBLANES
    for r in range(1, SUBLANES):
        ext_ref[r, 0:shifted_rows, :] = ext_ref[0, r:r + shifted_rows, :]
    lead = CONV_HALO - (CONV_WIDTH - 1)
    for r0 in range(0, tm, CONV_ROWS):
        acc = jnp.zeros((CONV_ROWS // SUBLANES, SUBLANES, CONV_CH), F32)
        for w in range(CONV_WIDTH):
            r = (lead + w) % SUBLANES
            a = lead + w - r + r0
            acc = acc + ext_ref[r, a:a + CONV_ROWS, :].reshape(acc.shape) * wdw_ref[w]
        cv_ref[r0:r0 + CONV_ROWS, :] = acc.reshape(CONV_ROWS, CONV_CH) + bdw_ref[...]
    x1_ref[0] = _mix_tail(cv_ref[...], gated_attn, sgc, x, gt1_ref[0], gln_ref[...], bln_ref[...], wcp_ref, wo_ref)


def _mix_prompt(x3, attn, u, mod3, g_norm1, wdw, bdw, gln, bln, w_in_b, wap_b, wcp_b, wo_b, *, tm):
    b, s, _ = x3.shape
    nt = s // tm
    halo_per_tile = tm // CONV_HALO
    tok = lambda width: pl.BlockSpec((1, tm, width), lambda bi, t: (bi, t, 0))
    full = lambda shape: pl.BlockSpec(shape, lambda bi, t: (0,) * len(shape))
    mod_spec = lambda chunk: pl.BlockSpec((1, 1, D_MODEL), lambda bi, t: (bi, 0, chunk))
    return pl.pallas_call(
        _mix_prompt_kernel,
        grid=(b, nt),
        in_specs=[
            tok(D_MODEL), tok(ATTN_WIDTH), tok(CONV_CH),
            pl.BlockSpec((1, CONV_HALO, CONV_CH), lambda bi, t: (bi, jnp.maximum(t * halo_per_tile - 1, 0), 0)),
            mod_spec(0), mod_spec(1), mod_spec(2),
            full((1, D_MODEL)),
            full(wdw.shape), full((1, CONV_CH)), full((1, CONV_CH)), full((1, CONV_CH)),
            *_gate_weight_specs(), full(wap_b.shape), full(wcp_b.shape), full(wo_b.shape),
        ],
        out_specs=tok(D_MODEL),
        out_shape=jax.ShapeDtypeStruct((b, s, D_MODEL), F32),
        scratch_shapes=[pltpu.VMEM((SUBLANES, CONV_HALO + tm, CONV_CH), F32), pltpu.VMEM((tm, CONV_CH), F32)],
        compiler_params=_params(("arbitrary", "arbitrary")),
        name="mix_prompt",
    )(x3, attn, u, u, mod3, mod3, mod3, g_norm1.reshape(1, D_MODEL), wdw, bdw, gln, bln,
      *([w_in_b] * len(_gate_weight_specs())), wap_b, wcp_b, wo_b)


def _mix_sample_kernel(x_ref, attn_ref, u_ref, st_ref, sh1_ref, sc1_ref, gt1_ref, g1_ref, wdw_ref, bdw_ref,
                       gln_ref, bln_ref, wg0_ref, wg1_ref, wg2_ref, wg3_ref, wap_ref, wcp_ref, wo_ref,
                       x1_ref, st_out_ref):
    u = u_ref[...]
    past = CONV_WIDTH - 1
    cv = u * wdw_ref[past:past + 1, :] + bdw_ref[...]
    for w in range(past):
        cv = cv + st_ref[w] * wdw_ref[w:w + 1, :]
    for w in range(1, past):
        st_out_ref[w - 1] = st_ref[w]
    st_out_ref[past - 1] = u
    x = x_ref[...]
    gated_attn, sgc = _mix_head(attn_ref[...].astype(BF16), x, sh1_ref[...], sc1_ref[...], g1_ref[...],
                                (wg0_ref, wg1_ref, wg2_ref, wg3_ref), wap_ref)
    x1_ref[...] = _mix_tail(cv, gated_attn, sgc, x, gt1_ref[...], gln_ref[...], bln_ref[...], wcp_ref, wo_ref)


def _mix_sample(x2, attn, u, st_t, mod_s, g_norm1, wdw, bdw, gln, bln, w_in_b, wap_b, wcp_b, wo_b):
    n = x2.shape[0]
    full = lambda shape: pl.BlockSpec(shape, lambda j: (0,) * len(shape))
    mod_spec = lambda chunk: pl.BlockSpec((n, D_MODEL), lambda j: (0, chunk))
    return pl.pallas_call(
        _mix_sample_kernel,
        grid=(1,),
        in_specs=[
            full(x2.shape), full(attn.shape), full(u.shape), full(st_t.shape),
            mod_spec(0), mod_spec(1), mod_spec(2),
            full((1, D_MODEL)),
            full(wdw.shape), full((1, CONV_CH)), full((1, CONV_CH)), full((1, CONV_CH)),
            *_gate_weight_specs(), full(wap_b.shape), full(wcp_b.shape), full(wo_b.shape),
        ],
        out_specs=[full(x2.shape), full(st_t.shape)],
        out_shape=[jax.ShapeDtypeStruct(x2.shape, F32), jax.ShapeDtypeStruct(st_t.shape, F32)],
        compiler_params=_params(("arbitrary",)),
        name="mix_sample",
    )(x2, attn, u, st_t, mod_s, mod_s, mod_s, g_norm1.reshape(1, D_MODEL), wdw, bdw, gln, bln,
      *([w_in_b] * len(_gate_weight_specs())), wap_b, wcp_b, wo_b)


def _ffn_kernel(*refs, host):
    if host is not None:
        pt_ref, refs = refs[0], refs[1:]
        qs_ref, kt_hbm, y_ref, gate_ref, page_buf, page_sem = refs[7:]
        slot = host.fetch(pt_ref, kt_hbm, page_buf, page_sem)
    else:
        y_ref = refs[7]
    x_ref, sh_ref, sc_ref, gt_ref, g2_ref, wi_ref, wo_ref = refs[:7]
    x = x_ref[0]
    hb = _mod_rms_norm(x, g2_ref[...], sc_ref[0], sh_ref[0]).astype(BF16)
    acc = jnp.zeros(x.shape, F32)
    for c in range(D_FF // FFN_CHUNK):
        c0 = c * FFN_CHUNK
        f_g = _dot(hb, wi_ref[:, c0:c0 + FFN_CHUNK])
        f_u = _dot(hb, wi_ref[:, D_FF + c0:D_FF + c0 + FFN_CHUNK])
        acc = acc + _dot((_silu(f_g) * f_u).astype(BF16), wo_ref[c0:c0 + FFN_CHUNK, :])
    y_ref[0] = x + gt_ref[0] * acc
    if host is not None:
        _host_gate(qs_ref, page_buf.at[slot], gate_ref, host)


def _ffn(x3, mod3, g_norm2, wi_b, wo_b, *, tm, name, gate_host=None):
    g, s, _ = x3.shape
    r = mod3.shape[1]
    nt = s // tm
    host = _GateHost(*gate_host, grid=(g, nt)) if gate_host is not None else None

    def mod_spec(chunk):
        if r == 1:
            return pl.BlockSpec((1, 1, D_MODEL), lambda b, t, *_: (b, 0, chunk))
        return pl.BlockSpec((1, tm, D_MODEL), lambda b, t, *_: (b, t, chunk))

    tok = pl.BlockSpec((1, tm, D_MODEL), lambda b, t, *_: (b, t, 0))
    in_specs = [tok, mod_spec(3), mod_spec(4), mod_spec(5),
                _resident((1, D_MODEL)), _resident(wi_b.shape), _resident(wo_b.shape)]
    operands = [x3, mod3, mod3, mod3, g_norm2.reshape(1, D_MODEL), wi_b, wo_b]
    out_specs = [tok]
    out_shape = [jax.ShapeDtypeStruct(x3.shape, F32)]
    prefetch = []
    if host is not None:
        in_specs += host.in_specs()
        operands += host.operands()
        out_specs += [host.out_spec()]
        out_shape += [host.out_shape()]
        prefetch = [host.page_table]
    grid_spec = pltpu.PrefetchScalarGridSpec(
        num_scalar_prefetch=len(prefetch), grid=(g, nt), in_specs=in_specs, out_specs=out_specs,
        scratch_shapes=host.scratch_shapes() if host is not None else [])
    return pl.pallas_call(
        functools.partial(_ffn_kernel, host=host),
        grid_spec=grid_spec,
        out_shape=out_shape,
        compiler_params=_params(("arbitrary", "arbitrary")),
        name=name,
    )(*prefetch, *operands)


HOST_PAGES = 32
HOST_BLOCKS = HOST_PAGES // PAGES_PER_BLOCK


def _host_gate(qs_ref, pages_ref, gate_ref, host):
    qcol = jnp.broadcast_to(qs_ref[0], (LANES, ATTN_WIDTH)).T
    chunk = host.chunk(pl.program_id(0), pl.program_id(1))
    sub = _iota((N_HEADS, LANES), 0)
    lane = _iota((N_HEADS, LANES), 1)
    gate = jnp.zeros((N_HEADS, LANES), F32)
    for h in range(N_HEADS):
        q_h = qcol[h * HEAD_DIM:(h + 1) * HEAD_DIM]
        col = jnp.zeros((SUBLANES, LANES), F32)
        for blk in range(HOST_BLOCKS):
            kt = pages_ref[blk * PAGES_PER_BLOCK, h]
            for pg in range(1, PAGES_PER_BLOCK):
                kt = kt + pages_ref[blk * PAGES_PER_BLOCK + pg, h]
            part = jnp.sum((kt * q_h).reshape(HEAD_DIM // SUBLANES, SUBLANES, PAGE_SIZE), axis=0)
            col = jnp.where(lane == chunk * HOST_BLOCKS + blk, jnp.sum(part, axis=1, keepdims=True), col)
        row = jnp.sum(col, axis=0, keepdims=True) * (1.0 / MOBA_BLOCK)
        gate = jnp.where(sub == h, row, gate)
    gate_ref[0, 0] = gate


class _GateHost:
    def __init__(self, page_table, q_s3, cache_kt, seq0, grid):
        n_pages = page_table.shape[1]
        self.chunks = n_pages // HOST_PAGES
        g, nt = grid
        self.nseq = g * nt // self.chunks
        assert g * nt == self.nseq * self.chunks and seq0 + self.nseq <= page_table.shape[0]
        self.page_table, self.q_s3, self.cache_kt, self.seq0, self.nt = page_table, q_s3, cache_kt, seq0, nt

    def _seq(self, b, t):
        return self.seq0 + (b * self.nt + t) // self.chunks

    def chunk(self, b, t):
        return (b * self.nt + t) % self.chunks

    def in_specs(self):
        return [pl.BlockSpec((1, 1, ATTN_WIDTH), lambda b, t, pt: (self._seq(b, t), 0, 0)),
                pl.BlockSpec(memory_space=pl.ANY)]

    def operands(self):
        return [self.q_s3, self.cache_kt]

    def scratch_shapes(self):
        return [pltpu.VMEM((2, HOST_PAGES) + self.cache_kt.shape[1:], F32), pltpu.SemaphoreType.DMA((2,))]

    def _copies(self, pt_ref, kt_hbm, page_buf, page_sem, step, slot, for_wait):
        seq = self.seq0 + step // self.chunks
        first = (step % self.chunks) * HOST_PAGES
        return [pltpu.make_async_copy(kt_hbm.at[0 if for_wait else pt_ref[seq, first + r]],
                                      page_buf.at[slot, r], page_sem.at[slot]) for r in range(HOST_PAGES)]

    def fetch(self, pt_ref, kt_hbm, page_buf, page_sem):
        step = pl.program_id(0) * self.nt + pl.program_id(1)
        n_steps = pl.num_programs(0) * self.nt
        slot = step % 2

        @pl.when(step == 0)
        def _():
            for c in self._copies(pt_ref, kt_hbm, page_buf, page_sem, step, slot, False):
                c.start()

        @pl.when(step + 1 < n_steps)
        def _():
            for c in self._copies(pt_ref, kt_hbm, page_buf, page_sem, step + 1, 1 - slot, False):
                c.start()

        for c in self._copies(pt_ref, kt_hbm, page_buf, page_sem, step, slot, True):
            c.wait()
        return slot

    def out_spec(self):
        return pl.BlockSpec((1, 1, N_HEADS, LANES), lambda b, t, pt: (self._seq(b, t) - self.seq0, self.chunk(b, t), 0, 0))

    def out_shape(self):
        return jax.ShapeDtypeStruct((self.nseq, self.chunks, N_HEADS, LANES), F32)


def _topk_kernel(gp_ref, idx_ref):
    chunks = gp_ref.shape[1]
    g = gp_ref[:, 0]
    for c in range(1, chunks):
        g = g + gp_ref[:, c]
    lane = _iota(g.shape, 2)
    g = jnp.where(lane < chunks * HOST_BLOCKS, g, -jnp.inf)
    out = jnp.zeros(g.shape, jnp.int32)
    for r in range(MOBA_TOPK):
        best = jnp.max(g, axis=2, keepdims=True)
        pick = jnp.min(jnp.where(g == best, lane, LANES), axis=2, keepdims=True)
        out = jnp.where(lane == r, pick, out)
        g = jnp.where(lane == pick, -jnp.inf, g)
    idx_ref[...] = out


def _topk(gate_parts):
    nseq = gate_parts.shape[0]
    out_shape = (nseq, N_HEADS, LANES)
    return pl.pallas_call(
        _topk_kernel,
        grid=(1,),
        in_specs=[pl.BlockSpec(gate_parts.shape, lambda j: (0, 0, 0, 0))],
        out_specs=pl.BlockSpec(out_shape, lambda j: (0, 0, 0)),
        out_shape=jax.ShapeDtypeStruct(out_shape, jnp.int32),
        compiler_params=_params(("arbitrary",)),
        name="moba_topk_sample",
    )(gate_parts)


SEL_PAGES = MOBA_TOPK * PAGES_PER_BLOCK
SATTN_HEADS_PER_STEP = 4


def _sattn_kernel(pt_ref, idx_ref, q_ref, kn_ref, vn_ref, *refs, past_len):
    del pt_ref
    n_pg = SATTN_HEADS_PER_STEP * SEL_PAGES
    k_refs = refs[:n_pg]
    v_refs = refs[n_pg:2 * n_pg]
    o_ref = refs[2 * n_pg]
    b = pl.program_id(0)
    hg = pl.program_id(1)
    lane = _iota((1, PAGE_SIZE), 1)
    eye = _iota((HEAD_DIM, HEAD_DIM), 0) == _iota((HEAD_DIM, HEAD_DIM), 1)
    for hh in range(SATTN_HEADS_PER_STEP):
        h = hg * SATTN_HEADS_PER_STEP + hh
        q_row = q_ref[0, hh] * SM_SCALE
        q = jnp.sum(jnp.where(eye, q_row, 0.0), axis=1, keepdims=True)
        k_cat = jnp.concatenate([k_refs[hh * SEL_PAGES + j][0, 0] for j in range(SEL_PAGES)], axis=1)
        v_cat = jnp.concatenate([v_refs[hh * SEL_PAGES + j][0, 0] for j in range(SEL_PAGES)], axis=1)
        pos = []
        for r in range(MOBA_TOPK):
            blk = idx_ref[(b * N_HEADS + h) * MOBA_TOPK + r]
            pos += [blk * MOBA_BLOCK + pg * PAGE_SIZE + lane for pg in range(PAGES_PER_BLOCK)]
        dist = (past_len - jnp.concatenate(pos, axis=1)).astype(F32)
        slope = lax.bitcast_convert_type(jnp.full(dist.shape, 126, jnp.int32) - h << 23, F32)
        s = jnp.sum(k_cat * q, axis=0, keepdims=True) - slope * dist
        s_own = jnp.sum(q_row * kn_ref[0, hh], axis=1, keepdims=True)
        m = jnp.maximum(jnp.max(s, axis=1, keepdims=True), s_own)
        p = jnp.exp(s - m)
        p_own = jnp.exp(s_own - m)
        l = jnp.sum(p, axis=1, keepdims=True) + p_own
        acc = jnp.sum(v_cat * p, axis=1, keepdims=True)
        acc_row = jnp.sum(jnp.where(eye, acc, 0.0), axis=0, keepdims=True)
        o_ref[0, hh] = (acc_row + p_own * vn_ref[0, hh]) / l


def _sattn(page_table, idx_flat, q4, kn4, vn4, cache_kt, cache_vt):
    nseq, n_pages = page_table.shape
    past_len = n_pages * PAGE_SIZE
    hps = SATTN_HEADS_PER_STEP

    def page_spec(hh, r, pg):
        def index_map(b, hg, pt, idx):
            h = hg * hps + hh
            return (pt[b, idx[(b * N_HEADS + h) * MOBA_TOPK + r] * PAGES_PER_BLOCK + pg], h, 0, 0)
        return pl.BlockSpec((1, 1, HEAD_DIM, PAGE_SIZE), index_map)

    tok = pl.BlockSpec((1, hps, 1, HEAD_DIM), lambda b, hg, pt, idx: (b, hg, 0, 0))
    pages = [page_spec(hh, r, pg) for hh in range(hps) for r in range(MOBA_TOPK) for pg in range(PAGES_PER_BLOCK)]
    grid_spec = pltpu.PrefetchScalarGridSpec(
        num_scalar_prefetch=2,
        grid=(nseq, N_HEADS // hps),
        in_specs=[tok, tok, tok] + pages + pages,
        out_specs=tok,
    )
    return pl.pallas_call(
        functools.partial(_sattn_kernel, past_len=past_len),
        grid_spec=grid_spec,
        out_shape=jax.ShapeDtypeStruct((nseq, N_HEADS, 1, HEAD_DIM), F32),
        compiler_params=_params(("arbitrary", "arbitrary")),
        name="moba_attn_sample",
    )(page_table, idx_flat, q4, kn4, vn4, *([cache_kt] * len(pages)), *([cache_vt] * len(pages)))


def kernel(x_prompt, x_sample, cache_k, cache_v, state_conv, page_table, c_prompt, c_sample, w_ada, b_ada, g_norm1, w_in, g_q, g_k, w_attn_proj, w_dwconv, b_dwconv, g_conv_ln, b_conv_ln, w_conv_proj, w_out, g_norm2, w_ffn_in, w_ffn_out):
    nb, seq, _ = x_prompt.shape
    ns = x_sample.shape[0]

    w_in_b = w_in.astype(BF16)
    wap_b = w_attn_proj.astype(BF16)
    wcp_b = w_conv_proj.astype(BF16)
    wo_b = w_out.astype(BF16)
    wfi_b = w_ffn_in.astype(BF16)
    wfo_b = w_ffn_out.astype(BF16)
    gq_t = jnp.tile(g_q, N_HEADS).reshape(1, ATTN_WIDTH)
    gk_t = jnp.tile(g_k, N_HEADS).reshape(1, ATTN_WIDTH)
    wdw = w_dwconv.reshape(CONV_WIDTH, CONV_CH)
    bdw = b_dwconv.reshape(1, CONV_CH)
    gln = g_conv_ln.reshape(1, CONV_CH)
    bln = b_conv_ln.reshape(1, CONV_CH)

    mod_p, mod_s = _ada(c_prompt, c_sample, w_ada, b_ada)
    mod_p3 = mod_p.reshape(nb, 1, 6 * D_MODEL)
    mod_s3 = mod_s.reshape(1, ns, 6 * D_MODEL)

    xs3 = x_sample.reshape(1, ns, D_MODEL)
    qs, ks, vs, us = _inproj(xs3, mod_s3, g_norm1, w_in_b, gq_t, gk_t, tm=ns, wq32=w_in)
    qs3 = qs.reshape(ns, 1, ATTN_WIDTH)
    cache_kt = jnp.transpose(cache_k, (0, 2, 3, 1))
    cache_vt = jnp.transpose(cache_v, (0, 2, 3, 1))

    q, k, v, kb, vt, u, km, gates_a = _inproj(x_prompt, mod_p3, g_norm1, w_in_b, gq_t, gk_t, tm=256,
                                              gate_host=(page_table, qs3, cache_kt, 0))
    attn = _attn(q, kb, vt, km.reshape(nb, seq // MOBA_BLOCK, ATTN_WIDTH))
    wdw8 = jnp.broadcast_to(wdw[:, None, :], (CONV_WIDTH, SUBLANES, CONV_CH))
    x1 = _mix_prompt(x_prompt, attn, u, mod_p3, g_norm1, wdw8, bdw, gln, bln, w_in_b, wap_b, wcp_b, wo_b, tm=256)
    y_prompt, gates_b = _ffn(x1, mod_p3, g_norm2, wfi_b, wfo_b, tm=256, name="ffn_prompt",
                             gate_host=(page_table, qs3, cache_kt, gates_a.shape[0]))
    assert gates_a.shape[0] + gates_b.shape[0] == ns
    k_prompt = k.reshape(nb, seq, N_HEADS, HEAD_DIM)
    v_prompt = v.reshape(nb, seq, N_HEADS, HEAD_DIM)
    conv_prompt = u[:, seq - (CONV_WIDTH - 1):, :]

    idx = _topk(jnp.concatenate([gates_a, gates_b], axis=0))
    idx_flat = idx[:, :, :MOBA_TOPK].reshape(-1)
    head4 = lambda t: t.reshape(ns, N_HEADS, 1, HEAD_DIM)
    attn_s = _sattn(page_table, idx_flat, head4(qs), head4(ks), head4(vs), cache_kt, cache_vt)
    x1_s, st_new = _mix_sample(
        xs3[0], attn_s.reshape(ns, ATTN_WIDTH), us[0], jnp.transpose(state_conv, (1, 0, 2)),
        mod_s, g_norm1, wdw, bdw, gln, bln, w_in_b, wap_b, wcp_b, wo_b)
    (y_sample,) = _ffn(x1_s.reshape(1, ns, D_MODEL), mod_s3, g_norm2, wfi_b, wfo_b, tm=ns, name="ffn_sample")

    return (y_prompt, y_sample.reshape(ns, 1, D_MODEL), k_prompt, v_prompt, conv_prompt,
            ks.reshape(ns, 1, N_HEADS, HEAD_DIM), vs.reshape(ns, 1, N_HEADS, HEAD_DIM),
            jnp.transpose(st_new, (1, 0, 2)))
```

```python
import functools

import jax
import jax.numpy as jnp
from jax import lax
from jax.experimental import pallas as pl
from jax.experimental.pallas import tpu as pltpu

D_MODEL = 1024
N_HEADS = 8
HEAD_DIM = 64
ATTN_WIDTH = N_HEADS * HEAD_DIM
CONV_CH = 512
CONV_WIDTH = 31
CONV_HALO = 32
CONV_ROWS = 32
MOBA_BLOCK = 256
MOBA_TOPK = 3
PAGE_SIZE = 128
PAGES_PER_BLOCK = MOBA_BLOCK // PAGE_SIZE
N_QKVU = 3 * ATTN_WIDTH + 2 * CONV_CH
GATE_COLS = 512
D_FF = 2816
FFN_CHUNK = 256
CAST_ROWS = 256
EPS = 1e-6
NEG = -1e30
BIG = 1e30
SM_SCALE = HEAD_DIM ** -0.5
LOG2E = 1.4426950408889634

LANES = 128
SUBLANES = 8
BF16_ROWS = 16
HEADS_PER_LANE_TILE = LANES // HEAD_DIM
VMEM_LIMIT = 56 * 1024 * 1024

F32 = jnp.float32
BF16 = jnp.bfloat16
HIGHEST = lax.Precision.HIGHEST


def _sigmoid(x):
    return 0.5 * jnp.tanh(0.5 * x) + 0.5


def _silu(x):
    return x * _sigmoid(x)


def _dot(a, b, precision=None):
    return jnp.dot(a, b, precision=precision, preferred_element_type=F32)


def _dot_nt(a, b):
    return lax.dot_general(a, b, (((1,), (1,)), ((), ())), preferred_element_type=F32)


def _iota(shape, dim):
    return lax.broadcasted_iota(jnp.int32, shape, dim)


def _alibi_slope(h):
    return 2.0 ** -(h + 1)


def _mod_rms_norm(x, gain, scale, shift):
    ms = jnp.mean(x * x, axis=-1, keepdims=True)
    return x * lax.rsqrt(ms + EPS) * gain * (1.0 + scale) + shift


def _resident(shape):
    return pl.BlockSpec(shape, lambda *_: (0,) * len(shape), pipeline_mode=pl.Buffered(1))


def _weight_cols(width, col_block):
    return pl.BlockSpec((D_MODEL, width), lambda *_: (0, col_block), pipeline_mode=pl.Buffered(1))


def _params(semantics):
    return pltpu.CompilerParams(dimension_semantics=semantics, vmem_limit_bytes=VMEM_LIMIT)


def _cast_kernel(w_ref, o_ref):
    o_ref[...] = w_ref[...].astype(BF16)


def _to_bf16(w):
    rows, cols = w.shape
    spec = pl.BlockSpec((CAST_ROWS, cols), lambda i: (i, 0))
    return pl.pallas_call(
        _cast_kernel,
        grid=(rows // CAST_ROWS,),
        in_specs=[spec],
        out_specs=spec,
        out_shape=jax.ShapeDtypeStruct(w.shape, BF16),
        compiler_params=_params(("arbitrary",)),
        name="weights_to_bf16",
    )(w)


def _ada_kernel(cp_ref, cs_ref, w_ref, b_ref, mp_ref, ms_ref):
    w = w_ref[...].astype(BF16)
    b = b_ref[...]
    mp_ref[...] = _dot(_silu(cp_ref[...]).astype(BF16), w) + b
    ms_ref[...] = _dot(_silu(cs_ref[...]).astype(BF16), w) + b


def _ada(c_prompt, c_sample, w_ada, b_ada):
    nb, ns = c_prompt.shape[0], c_sample.shape[0]
    n_out = w_ada.shape[1]
    tn = D_MODEL
    return pl.pallas_call(
        _ada_kernel,
        grid=(n_out // tn,),
        in_specs=[
            pl.BlockSpec((nb, D_MODEL), lambda j: (0, 0)),
            pl.BlockSpec((ns, D_MODEL), lambda j: (0, 0)),
            pl.BlockSpec((D_MODEL, tn), lambda j: (0, j)),
            pl.BlockSpec((1, tn), lambda j: (0, j)),
        ],
        out_specs=[
            pl.BlockSpec((nb, tn), lambda j: (0, j)),
            pl.BlockSpec((ns, tn), lambda j: (0, j)),
        ],
        out_shape=[jax.ShapeDtypeStruct((nb, n_out), F32), jax.ShapeDtypeStruct((ns, n_out), F32)],
        compiler_params=_params(("arbitrary",)),
        name="ada_mod",
    )(c_prompt, c_sample, w_ada, b_ada.reshape(1, n_out))


def _head_rms_norm(t, gain, group_ones):
    sq = t * t
    hi = sq.astype(BF16)
    lo = (sq - hi.astype(F32)).astype(BF16)
    ss = _dot(hi, group_ones) + _dot(lo, group_ones)
    return t * lax.rsqrt(ss * (1.0 / HEAD_DIM) + EPS) * gain


def _inproj_kernel(*refs, host):
    if host is not None:
        pt_ref, x_ref, sh_ref, sc_ref, g1_ref, w_ref, gq_ref, gk_ref = refs[:8]
        qs_refs, kt_hbm = refs[8:8 + host.cps], refs[8 + host.cps]
        q_ref, k_ref, v_ref, kb_ref, vt_ref, u_ref, km_ref, gate_ref, page_buf, page_sem = refs[9 + host.cps:]
        slot = host.fetch(pt_ref, kt_hbm, page_buf, page_sem)
    else:
        x_ref, sh_ref, sc_ref, g1_ref, w_ref, gq_ref, gk_ref, wq32_ref, q_ref, k_ref, v_ref, u_ref = refs
    prompt = host is not None
    h = _mod_rms_norm(x_ref[0], g1_ref[...], sc_ref[0], sh_ref[0])
    hb = h.astype(BF16)

    def proj(c0, c1):
        return _dot(hb, w_ref[:, c0:c1])

    a, c = ATTN_WIDTH, CONV_CH
    row = _iota((a, a), 0)
    col = _iota((a, a), 1)
    group_ones = jnp.where((row >> 6) == (col >> 6), 1.0, 0.0).astype(BF16)

    q_pre = proj(0, a) if prompt else _dot(h, wq32_ref[...], HIGHEST)
    q = _head_rms_norm(q_pre, gq_ref[...], group_ones)
    k = _head_rms_norm(proj(a, 2 * a), gk_ref[...], group_ones)
    v = proj(2 * a, 3 * a)
    q_ref[0] = q
    k_ref[0] = k
    v_ref[0] = v
    glu_a = proj(3 * a, 3 * a + c)
    glu_b = proj(3 * a + c, 3 * a + 2 * c)
    u_ref[0] = glu_a * _sigmoid(glu_b)
    if prompt:
        for r in range(k.shape[0] // MOBA_BLOCK):
            rows = slice(r * MOBA_BLOCK, (r + 1) * MOBA_BLOCK)
            kb_ref[0, r] = k[rows].astype(BF16)
            vt_ref[0, r] = v[rows].T.astype(BF16)
            km_ref[0, r] = jnp.mean(k[rows], axis=0, keepdims=True)
        host.reduce(qs_refs, page_buf.at[slot], gate_ref)


def _inproj(x3, mod3, g_norm1, w_in_b, gq_t, gk_t, *, tm, wq32=None, gate_host=None):
    g, s, _ = x3.shape
    r = mod3.shape[1]
    nt = s // tm
    prompt = gate_host is not None
    host = _GateHost(*gate_host, grid=(g, nt)) if prompt else None

    def mod_spec(chunk):
        if r == 1:
            return pl.BlockSpec((1, 1, D_MODEL), lambda b, t, *_: (b, 0, chunk))
        return pl.BlockSpec((1, tm, D_MODEL), lambda b, t, *_: (b, t, chunk))

    tok = lambda width: pl.BlockSpec((1, tm, width), lambda b, t, *_: (b, t, 0))
    in_specs = [tok(D_MODEL), mod_spec(0), mod_spec(1),
                _resident((1, D_MODEL)), _weight_cols(N_QKVU, 0), _resident((1, ATTN_WIDTH)), _resident((1, ATTN_WIDTH))]
    operands = [x3, mod3, mod3, g_norm1.reshape(1, D_MODEL), w_in_b, gq_t, gk_t]
    out_specs = [tok(ATTN_WIDTH), tok(ATTN_WIDTH), tok(ATTN_WIDTH)]
    out_shape = [jax.ShapeDtypeStruct((g, s, ATTN_WIDTH), F32)] * 3
    if prompt:
        nblk = tm // MOBA_BLOCK
        blk_spec = lambda rows, cols: pl.BlockSpec((1, nblk, rows, cols), lambda b, t, *_: (b, t, 0, 0))
        in_specs += host.in_specs()
        operands += host.operands()
        out_specs += [blk_spec(MOBA_BLOCK, ATTN_WIDTH), blk_spec(ATTN_WIDTH, MOBA_BLOCK), tok(CONV_CH),
                      blk_spec(1, ATTN_WIDTH), host.out_spec()]
        out_shape += [jax.ShapeDtypeStruct((g, s // MOBA_BLOCK, MOBA_BLOCK, ATTN_WIDTH), BF16),
                      jax.ShapeDtypeStruct((g, s // MOBA_BLOCK, ATTN_WIDTH, MOBA_BLOCK), BF16),
                      jax.ShapeDtypeStruct((g, s, CONV_CH), F32),
                      jax.ShapeDtypeStruct((g, s // MOBA_BLOCK, 1, ATTN_WIDTH), F32),
                      host.out_shape()]
        prefetch = [host.page_table]
    else:
        in_specs += [_weight_cols(ATTN_WIDTH, 0)]
        operands += [wq32]
        out_specs += [tok(CONV_CH)]
        out_shape += [jax.ShapeDtypeStruct((g, s, CONV_CH), F32)]
        prefetch = []
    grid_spec = pltpu.PrefetchScalarGridSpec(
        num_scalar_prefetch=len(prefetch), grid=(g, nt), in_specs=in_specs, out_specs=out_specs,
        scratch_shapes=host.scratch_shapes() if prompt else [])
    return pl.pallas_call(
        functools.partial(_inproj_kernel, host=host),
        grid_spec=grid_spec,
        out_shape=out_shape,
        compiler_params=_params(("arbitrary", "arbitrary")),
        name="in_proj_prompt" if prompt else "in_proj_sample",
    )(*prefetch, *operands)


def _moba_keep_t(q_t, km, q_blk, keep_ref):
    nb = km.shape[0]
    assert nb == SUBLANES
    kmb = jnp.concatenate([km] * N_HEADS, axis=0)
    row = _iota(kmb.shape, 0)
    col = _iota(kmb.shape, 1)
    kmb = jnp.where((row >> 3) == (col >> 6), kmb, 0.0)
    gate_t = _dot(kmb, q_t, HIGHEST)
    n_idx = _iota((nb, q_t.shape[1]), 0)
    for h in range(N_HEADS):
        gate = gate_t[h * nb:(h + 1) * nb]
        beaten_by = jnp.zeros(gate.shape, jnp.int32)
        for s in range(1, nb):
            g_m = pltpu.roll(gate, s, axis=0)
            lower = n_idx >= s
            m_idx = jnp.where(lower, n_idx - s, n_idx - s + nb)
            beats = (g_m > gate) | (lower & (g_m == gate))
            beaten_by += jnp.where(beats & (m_idx < q_blk), 1, 0)
        keep = (n_idx < q_blk) & (beaten_by < MOBA_TOPK)
        keep_ref[h * nb:(h + 1) * nb, :] = jnp.where(keep, 1.0, 0.0)


def _attn_kernel(q_ref, kb_ref, vt_ref, km_ref, o_ref, qz_ref, keep_ref, bias_ref, m_ref, accl_ref, x_ref, p_ref):
    i = pl.program_id(1)
    nb = km_ref.shape[1]
    blk = MOBA_BLOCK
    q_t = q_ref[0].T
    _moba_keep_t(q_t, km_ref[0], i, keep_ref)

    key_r = _iota((blk, blk), 0)
    qry_c = _iota((blk, blk), 1)
    rel_t = (qry_c - key_r).astype(F32)
    causal = qry_c >= key_r
    qz_t = q_t * (SM_SCALE * LOG2E)
    pair_row = _iota((LANES, blk), 0)
    ones_rows = jnp.ones((BF16_ROWS, blk), BF16)
    for h in range(N_HEADS):
        hp, half = divmod(h, HEADS_PER_LANE_TILE)
        bias_ref[h] = rel_t * (-_alibi_slope(h) * LOG2E)
        qz_ref[h] = jnp.where((pair_row >> 6) == half, qz_t[hp * LANES:(hp + 1) * LANES], 0.0).astype(BF16)

    def scores(jb):
        for h in range(N_HEADS):
            hp = h // HEADS_PER_LANE_TILE
            x_ref[h] = _dot(kb_ref[0, jb, :, hp * LANES:(hp + 1) * LANES], qz_ref[h]) + bias_ref[h]

    def values(h, jb):
        return jnp.concatenate([vt_ref[0, jb, h * HEAD_DIM:(h + 1) * HEAD_DIM, :], ones_rows], axis=0)

    scores(i)
    for h in range(N_HEADS):
        x = jnp.where(causal, x_ref[h], NEG)
        m = jnp.max(x, axis=0, keepdims=True)
        p_ref[h] = jnp.exp2(x - m).astype(BF16)
        m_ref[h:h + 1, :] = m
    for h in range(N_HEADS):
        accl_ref[h] = _dot(values(h, i), p_ref[h])

    def past_block(jb, carry):
        offset = ((i - jb) * blk).astype(F32)
        scores(jb)
        alphas = []
        for h in range(N_HEADS):
            x = x_ref[h]
            c_j = offset * (_alibi_slope(h) * LOG2E)
            kept = keep_ref[pl.ds(h * nb + jb, 1), :] > 0.5
            m_old = m_ref[h:h + 1, :]
            m_new = jnp.maximum(m_old, jnp.where(kept, jnp.max(x, axis=0, keepdims=True) - c_j, NEG))
            alphas.append(jnp.exp2(m_old - m_new))
            p_ref[h] = jnp.exp2(x - jnp.where(kept, m_new + c_j, BIG)).astype(BF16)
            m_ref[h:h + 1, :] = m_new
        for h in range(N_HEADS):
            accl_ref[h] = alphas[h] * accl_ref[h] + _dot(values(h, jb), p_ref[h])
        return carry

    lax.fori_loop(0, i, past_block, 0)

    outs = [accl_ref[h, 0:HEAD_DIM, :] / accl_ref[h, HEAD_DIM:HEAD_DIM + 1, :] for h in range(N_HEADS)]
    o_ref[0] = jnp.concatenate(outs, axis=0).T.astype(BF16)


def _attn(q, kb, vt, km):
    b, s, _ = q.shape
    nq = s // MOBA_BLOCK
    return pl.pallas_call(
        _attn_kernel,
        grid=(b, nq),
        in_specs=[
            pl.BlockSpec((1, MOBA_BLOCK, ATTN_WIDTH), lambda bi, i: (bi, i, 0)),
            pl.BlockSpec((1, nq, MOBA_BLOCK, ATTN_WIDTH), lambda bi, i: (bi, 0, 0, 0)),
            pl.BlockSpec((1, nq, ATTN_WIDTH, MOBA_BLOCK), lambda bi, i: (bi, 0, 0, 0)),
            pl.BlockSpec((1, nq, ATTN_WIDTH), lambda bi, i: (bi, 0, 0)),
        ],
        out_specs=pl.BlockSpec((1, MOBA_BLOCK, ATTN_WIDTH), lambda bi, i: (bi, i, 0)),
        out_shape=jax.ShapeDtypeStruct((b, s, ATTN_WIDTH), BF16),
        scratch_shapes=[
            pltpu.VMEM((N_HEADS, LANES, MOBA_BLOCK), BF16),
            pltpu.VMEM((N_HEADS * nq, MOBA_BLOCK), F32),
            pltpu.VMEM((N_HEADS, MOBA_BLOCK, MOBA_BLOCK), F32),
            pltpu.VMEM((N_HEADS, MOBA_BLOCK), F32),
            pltpu.VMEM((N_HEADS, HEAD_DIM + BF16_ROWS, MOBA_BLOCK), F32),
            pltpu.VMEM((N_HEADS, MOBA_BLOCK, MOBA_BLOCK), F32),
            pltpu.VMEM((N_HEADS, MOBA_BLOCK, MOBA_BLOCK), BF16),
        ],
        compiler_params=_params(("arbitrary", "arbitrary")),
        name="moba_attn_prompt",
    )(q, kb, vt, km)


def _gate_weight_specs():
    first = N_QKVU // GATE_COLS
    return [_weight_cols(GATE_COLS, first + j) for j in range(2 * D_MODEL // GATE_COLS)]


def _mix_head(attn_b, x, sh1, sc1, g1, wg_refs, wap_ref):
    hb = _mod_rms_norm(x, g1, sc1, sh1).astype(BF16)
    gates = [_dot(hb, w_ref[...]) for w_ref in wg_refs]
    per_gate = len(gates) // 2
    sga = _sigmoid(jnp.concatenate(gates[:per_gate], axis=1))
    sgc = _sigmoid(jnp.concatenate(gates[per_gate:], axis=1))
    return sga * _dot(attn_b, wap_ref[...]), sgc


def _mix_tail(cv, gated_attn, sgc, x, gt1, gln, bln, wcp_ref, wo_ref):
    mu = jnp.mean(cv, axis=-1, keepdims=True)
    cen = cv - mu
    var = jnp.mean(cen * cen, axis=-1, keepdims=True)
    y = cen * lax.rsqrt(var + EPS) * gln + bln
    cvp = _dot(_silu(y).astype(BF16), wcp_ref[...])
    mix = _dot((gated_attn + sgc * cvp).astype(BF16), wo_ref[...])
    return x + gt1 * mix


def _mix_prompt_kernel(x_ref, attn_ref, u_ref, halo_ref, sh1_ref, sc1_ref, gt1_ref, g1_ref, wdw_ref, bdw_ref,
                       gln_ref, bln_ref, wg0_ref, wg1_ref, wg2_ref, wg3_ref, wap_ref, wcp_ref, wo_ref,
                       x1_ref, ext_ref, cv_ref):
    t = pl.program_id(1)
    tm = u_ref.shape[1]
    x = x_ref[0]
    gated_attn, sgc = _mix_head(attn_ref[0], x, sh1_ref[0], sc1_ref[0], g1_ref[...],
                                (wg0_ref, wg1_ref, wg2_ref, wg3_ref), wap_ref)
    ext_ref[0, 0:CONV_HALO, :] = jnp.where(t > 0, halo_ref[0], 0.0)
    ext_ref[0, CONV_HALO:, :] = u_ref[0]
    shifted_rows = tm + CONV_HALO - SUBLANES
    for r in range(1, SUBLANES):
        ext_ref[r, 0:shifted_rows, :] = ext_ref[0, r:r + shifted_rows, :]
    lead = CONV_HALO - (CONV_WIDTH - 1)
    for r0 in range(0, tm, CONV_ROWS):
        acc = jnp.zeros((CONV_ROWS // SUBLANES, SUBLANES, CONV_CH), F32)
        for w in range(CONV_WIDTH):
            r = (lead + w) % SUBLANES
            a = lead + w - r + r0
            acc = acc + ext_ref[r, a:a + CONV_ROWS, :].reshape(acc.shape) * wdw_ref[w]
        cv_ref[r0:r0 + CONV_ROWS, :] = acc.reshape(CONV_ROWS, CONV_CH) + bdw_ref[...]
    x1_ref[0] = _mix_tail(cv_ref[...], gated_attn, sgc, x, gt1_ref[0], gln_ref[...], bln_ref[...], wcp_ref, wo_ref)


def _mix_prompt(x3, attn, u, mod3, g_norm1, wdw, bdw, gln, bln, w_in_b, wap_b, wcp_b, wo_b, *, tm):
    b, s, _ = x3.shape
    nt = s // tm
    halo_per_tile = tm // CONV_HALO
    tok = lambda width: pl.BlockSpec((1, tm, width), lambda bi, t: (bi, t, 0))
    full = lambda shape: pl.BlockSpec(shape, lambda bi, t: (0,) * len(shape))
    mod_spec = lambda chunk: pl.BlockSpec((1, 1, D_MODEL), lambda bi, t: (bi, 0, chunk))
    return pl.pallas_call(
        _mix_prompt_kernel,
        grid=(b, nt),
        in_specs=[
            tok(D_MODEL), tok(ATTN_WIDTH), tok(CONV_CH),
            pl.BlockSpec((1, CONV_HALO, CONV_CH), lambda bi, t: (bi, jnp.maximum(t * halo_per_tile - 1, 0), 0)),
            mod_spec(0), mod_spec(1), mod_spec(2),
            full((1, D_MODEL)),
            full(wdw.shape), full((1, CONV_CH)), full((1, CONV_CH)), full((1, CONV_CH)),
            *_gate_weight_specs(), full(wap_b.shape), full(wcp_b.shape), full(wo_b.shape),
        ],
        out_specs=tok(D_MODEL),
        out_shape=jax.ShapeDtypeStruct((b, s, D_MODEL), F32),
        scratch_shapes=[pltpu.VMEM((SUBLANES, CONV_HALO + tm, CONV_CH), F32), pltpu.VMEM((tm, CONV_CH), F32)],
        compiler_params=_params(("arbitrary", "arbitrary")),
        name="mix_prompt",
    )(x3, attn, u, u, mod3, mod3, mod3, g_norm1.reshape(1, D_MODEL), wdw, bdw, gln, bln,
      *([w_in_b] * len(_gate_weight_specs())), wap_b, wcp_b, wo_b)


def _mix_sample_kernel(x_ref, attn_ref, u_ref, st_ref, sh1_ref, sc1_ref, gt1_ref, g1_ref, wdw_ref, bdw_ref,
                       gln_ref, bln_ref, wg0_ref, wg1_ref, wg2_ref, wg3_ref, wap_ref, wcp_ref, wo_ref,
                       x1_ref, st_out_ref):
    u = u_ref[...]
    past = CONV_WIDTH - 1
    cv = u * wdw_ref[past:past + 1, :] + bdw_ref[...]
    for w in range(past):
        cv = cv + st_ref[w] * wdw_ref[w:w + 1, :]
    for w in range(1, past):
        st_out_ref[w - 1] = st_ref[w]
    st_out_ref[past - 1] = u
    x = x_ref[...]
    gated_attn, sgc = _mix_head(attn_ref[...].astype(BF16), x, sh1_ref[...], sc1_ref[...], g1_ref[...],
                                (wg0_ref, wg1_ref, wg2_ref, wg3_ref), wap_ref)
    x1_ref[...] = _mix_tail(cv, gated_attn, sgc, x, gt1_ref[...], gln_ref[...], bln_ref[...], wcp_ref, wo_ref)


def _mix_sample(x2, attn, u, st_t, mod_s, g_norm1, wdw, bdw, gln, bln, w_in_b, wap_b, wcp_b, wo_b):
    n = x2.shape[0]
    full = lambda shape: pl.BlockSpec(shape, lambda j: (0,) * len(shape))
    mod_spec = lambda chunk: pl.BlockSpec((n, D_MODEL), lambda j: (0, chunk))
    return pl.pallas_call(
        _mix_sample_kernel,
        grid=(1,),
        in_specs=[
            full(x2.shape), full(attn.shape), full(u.shape), full(st_t.shape),
            mod_spec(0), mod_spec(1), mod_spec(2),
            full((1, D_MODEL)),
            full(wdw.shape), full((1, CONV_CH)), full((1, CONV_CH)), full((1, CONV_CH)),
            *_gate_weight_specs(), full(wap_b.shape), full(wcp_b.shape), full(wo_b.shape),
        ],
        out_specs=[full(x2.shape), full(st_t.shape)],
        out_shape=[jax.ShapeDtypeStruct(x2.shape, F32), jax.ShapeDtypeStruct(st_t.shape, F32)],
        compiler_params=_params(("arbitrary",)),
        name="mix_sample",
    )(x2, attn, u, st_t, mod_s, mod_s, mod_s, g_norm1.reshape(1, D_MODEL), wdw, bdw, gln, bln,
      *([w_in_b] * len(_gate_weight_specs())), wap_b, wcp_b, wo_b)


def _ffn_kernel(*refs, host):
    if host is not None:
        pt_ref, refs = refs[0], refs[1:]
        qs_refs, kt_hbm = refs[7:7 + host.cps], refs[7 + host.cps]
        y_ref, gate_ref, page_buf, page_sem = refs[8 + host.cps:]
        slot = host.fetch(pt_ref, kt_hbm, page_buf, page_sem)
    else:
        y_ref = refs[7]
    x_ref, sh_ref, sc_ref, gt_ref, g2_ref, wi_ref, wo_ref = refs[:7]
    x = x_ref[0]
    hb = _mod_rms_norm(x, g2_ref[...], sc_ref[0], sh_ref[0]).astype(BF16)
    acc = jnp.zeros(x.shape, F32)
    for c in range(D_FF // FFN_CHUNK):
        c0 = c * FFN_CHUNK
        f_g = _dot(hb, wi_ref[:, c0:c0 + FFN_CHUNK])
        f_u = _dot(hb, wi_ref[:, D_FF + c0:D_FF + c0 + FFN_CHUNK])
        acc = acc + _dot((_silu(f_g) * f_u).astype(BF16), wo_ref[c0:c0 + FFN_CHUNK, :])
    y_ref[0] = x + gt_ref[0] * acc
    if host is not None:
        host.reduce(qs_refs, page_buf.at[slot], gate_ref)


def _ffn(x3, mod3, g_norm2, wi_b, wo_b, *, tm, name, gate_host=None):
    g, s, _ = x3.shape
    r = mod3.shape[1]
    nt = s // tm
    host = _GateHost(*gate_host, grid=(g, nt)) if gate_host is not None else None

    def mod_spec(chunk):
        if r == 1:
            return pl.BlockSpec((1, 1, D_MODEL), lambda b, t, *_: (b, 0, chunk))
        return pl.BlockSpec((1, tm, D_MODEL), lambda b, t, *_: (b, t, chunk))

    tok = pl.BlockSpec((1, tm, D_MODEL), lambda b, t, *_: (b, t, 0))
    in_specs = [tok, mod_spec(3), mod_spec(4), mod_spec(5),
                _resident((1, D_MODEL)), _resident(wi_b.shape), _resident(wo_b.shape)]
    operands = [x3, mod3, mod3, mod3, g_norm2.reshape(1, D_MODEL), wi_b, wo_b]
    out_specs = [tok]
    out_shape = [jax.ShapeDtypeStruct(x3.shape, F32)]
    prefetch = []
    if host is not None:
        in_specs += host.in_specs()
        operands += host.operands()
        out_specs += [host.out_spec()]
        out_shape += [host.out_shape()]
        prefetch = [host.page_table]
    grid_spec = pltpu.PrefetchScalarGridSpec(
        num_scalar_prefetch=len(prefetch), grid=(g, nt), in_specs=in_specs, out_specs=out_specs,
        scratch_shapes=host.scratch_shapes() if host is not None else [])
    return pl.pallas_call(
        functools.partial(_ffn_kernel, host=host),
        grid_spec=grid_spec,
        out_shape=out_shape,
        compiler_params=_params(("arbitrary", "arbitrary")),
        name=name,
    )(*prefetch, *operands)


HOST_CHUNK_PAGES = 16
HOST_BLOCKS = HOST_CHUNK_PAGES // PAGES_PER_BLOCK
IN_PROJ_HOST_CHUNKS = 1
FFN_HOST_CHUNKS = 3


def _chunk_gate(qs_ref, pages_ref, first_page, chunk_in_seq):
    qcol = jnp.broadcast_to(qs_ref[0], (LANES, ATTN_WIDTH)).T
    sub = _iota((N_HEADS, LANES), 0)
    lane = _iota((N_HEADS, LANES), 1)
    gate = jnp.zeros((N_HEADS, LANES), F32)
    for h in range(N_HEADS):
        q_h = qcol[h * HEAD_DIM:(h + 1) * HEAD_DIM]
        col = jnp.zeros((SUBLANES, LANES), F32)
        for blk in range(HOST_BLOCKS):
            kt = pages_ref[first_page + blk * PAGES_PER_BLOCK, h]
            for pg in range(1, PAGES_PER_BLOCK):
                kt = kt + pages_ref[first_page + blk * PAGES_PER_BLOCK + pg, h]
            part = jnp.sum((kt * q_h).reshape(HEAD_DIM // SUBLANES, SUBLANES, PAGE_SIZE), axis=0)
            col = jnp.where(lane == chunk_in_seq * HOST_BLOCKS + blk, jnp.sum(part, axis=1, keepdims=True), col)
        row = jnp.sum(col, axis=0, keepdims=True) * (1.0 / MOBA_BLOCK)
        gate = jnp.where(sub == h, row, gate)
    return gate


class _GateHost:
    def __init__(self, page_table, q_s3, cache_kt, first_chunk, chunks_per_step, grid):
        self.chunks_per_seq = page_table.shape[1] // HOST_CHUNK_PAGES
        self.cps = chunks_per_step
        self.steps = grid[0] * grid[1]
        self.nt = grid[1]
        assert first_chunk + self.steps * self.cps <= page_table.shape[0] * self.chunks_per_seq
        self.page_table, self.q_s3, self.cache_kt, self.first_chunk = page_table, q_s3, cache_kt, first_chunk

    def _chunk(self, step, j):
        return self.first_chunk + step * self.cps + j

    def in_specs(self):
        def q_spec(j):
            return pl.BlockSpec((1, 1, ATTN_WIDTH),
                                lambda b, t, pt: (self._chunk(b * self.nt + t, j) // self.chunks_per_seq, 0, 0))
        return [q_spec(j) for j in range(self.cps)] + [pl.BlockSpec(memory_space=pl.ANY)]

    def operands(self):
        return [self.q_s3] * self.cps + [self.cache_kt]

    def scratch_shapes(self):
        pages = self.cps * HOST_CHUNK_PAGES
        return [pltpu.VMEM((2, pages) + self.cache_kt.shape[1:], F32), pltpu.SemaphoreType.DMA((2,))]

    def _copies(self, pt_ref, kt_hbm, page_buf, page_sem, step, slot, for_wait):
        copies = []
        for j in range(self.cps):
            chunk = self._chunk(step, j)
            seq = chunk // self.chunks_per_seq
            first = (chunk % self.chunks_per_seq) * HOST_CHUNK_PAGES
            for r in range(HOST_CHUNK_PAGES):
                src = kt_hbm.at[0 if for_wait else pt_ref[seq, first + r]]
                dst = page_buf.at[slot, j * HOST_CHUNK_PAGES + r]
                copies.append(pltpu.make_async_copy(src, dst, page_sem.at[slot]))
        return copies

    def fetch(self, pt_ref, kt_hbm, page_buf, page_sem):
        step = pl.program_id(0) * self.nt + pl.program_id(1)
        slot = step % 2

        @pl.when(step == 0)
        def _():
            for c in self._copies(pt_ref, kt_hbm, page_buf, page_sem, step, slot, False):
                c.start()

        @pl.when(step + 1 < self.steps)
        def _():
            for c in self._copies(pt_ref, kt_hbm, page_buf, page_sem, step + 1, 1 - slot, False):
                c.start()

        for c in self._copies(pt_ref, kt_hbm, page_buf, page_sem, step, slot, True):
            c.wait()
        return slot

    def reduce(self, qs_refs, pages_ref, gate_ref):
        step = pl.program_id(0) * self.nt + pl.program_id(1)
        for j in range(self.cps):
            chunk_in_seq = self._chunk(step, j) % self.chunks_per_seq
            gate_ref[0, j] = _chunk_gate(qs_refs[j], pages_ref, j * HOST_CHUNK_PAGES, chunk_in_seq)

    def out_spec(self):
        return pl.BlockSpec((1, self.cps, N_HEADS, LANES), lambda b, t, pt: (b * self.nt + t, 0, 0, 0))

    def out_shape(self):
        return jax.ShapeDtypeStruct((self.steps, self.cps, N_HEADS, LANES), F32)


def _topk_kernel(gp_ref, idx_ref):
    chunks = gp_ref.shape[1]
    g = gp_ref[:, 0]
    for c in range(1, chunks):
        g = g + gp_ref[:, c]
    lane = _iota(g.shape, 2)
    g = jnp.where(lane < chunks * HOST_BLOCKS, g, -jnp.inf)
    out = jnp.zeros(g.shape, jnp.int32)
    for r in range(MOBA_TOPK):
        best = jnp.max(g, axis=2, keepdims=True)
        pick = jnp.min(jnp.where(g == best, lane, LANES), axis=2, keepdims=True)
        out = jnp.where(lane == r, pick, out)
        g = jnp.where(lane == pick, -jnp.inf, g)
    idx_ref[...] = out


def _topk(gate_parts):
    nseq = gate_parts.shape[0]
    out_shape = (nseq, N_HEADS, LANES)
    return pl.pallas_call(
        _topk_kernel,
        grid=(1,),
        in_specs=[pl.BlockSpec(gate_parts.shape, lambda j: (0, 0, 0, 0))],
        out_specs=pl.BlockSpec(out_shape, lambda j: (0, 0, 0)),
        out_shape=jax.ShapeDtypeStruct(out_shape, jnp.int32),
        compiler_params=_params(("arbitrary",)),
        name="moba_topk_sample",
    )(gate_parts)


SEL_PAGES = MOBA_TOPK * PAGES_PER_BLOCK
SATTN_HEADS_PER_STEP = 4


def _sattn_kernel(pt_ref, idx_ref, q_ref, kn_ref, vn_ref, *refs, past_len):
    del pt_ref
    n_pg = SATTN_HEADS_PER_STEP * SEL_PAGES
    k_refs = refs[:n_pg]
    v_refs = refs[n_pg:2 * n_pg]
    o_ref = refs[2 * n_pg]
    b = pl.program_id(0)
    hg = pl.program_id(1)
    lane = _iota((1, PAGE_SIZE), 1)
    eye = _iota((HEAD_DIM, HEAD_DIM), 0) == _iota((HEAD_DIM, HEAD_DIM), 1)
    for hh in range(SATTN_HEADS_PER_STEP):
        h = hg * SATTN_HEADS_PER_STEP + hh
        q_row = q_ref[0, hh] * SM_SCALE
        q = jnp.sum(jnp.where(eye, q_row, 0.0), axis=1, keepdims=True)
        k_cat = jnp.concatenate([k_refs[hh * SEL_PAGES + j][0, 0] for j in range(SEL_PAGES)], axis=1)
        v_cat = jnp.concatenate([v_refs[hh * SEL_PAGES + j][0, 0] for j in range(SEL_PAGES)], axis=1)
        pos = []
        for r in range(MOBA_TOPK):
            blk = idx_ref[(b * N_HEADS + h) * MOBA_TOPK + r]
            pos += [blk * MOBA_BLOCK + pg * PAGE_SIZE + lane for pg in range(PAGES_PER_BLOCK)]
        dist = (past_len - jnp.concatenate(pos, axis=1)).astype(F32)
        slope = lax.bitcast_convert_type(jnp.full(dist.shape, 126, jnp.int32) - h << 23, F32)
        s = jnp.sum(k_cat * q, axis=0, keepdims=True) - slope * dist
        s_own = jnp.sum(q_row * kn_ref[0, hh], axis=1, keepdims=True)
        m = jnp.maximum(jnp.max(s, axis=1, keepdims=True), s_own)
        p = jnp.exp(s - m)
        p_own = jnp.exp(s_own - m)
        l = jnp.sum(p, axis=1, keepdims=True) + p_own
        acc = jnp.sum(v_cat * p, axis=1, keepdims=True)
        acc_row = jnp.sum(jnp.where(eye, acc, 0.0), axis=0, keepdims=True)
        o_ref[0, hh] = (acc_row + p_own * vn_ref[0, hh]) / l


def _sattn(page_table, idx_flat, q4, kn4, vn4, cache_kt, cache_vt):
    nseq, n_pages = page_table.shape
    past_len = n_pages * PAGE_SIZE
    hps = SATTN_HEADS_PER_STEP

    def page_spec(hh, r, pg):
        def index_map(b, hg, pt, idx):
            h = hg * hps + hh
            return (pt[b, idx[(b * N_HEADS + h) * MOBA_TOPK + r] * PAGES_PER_BLOCK + pg], h, 0, 0)
        return pl.BlockSpec((1, 1, HEAD_DIM, PAGE_SIZE), index_map)

    tok = pl.BlockSpec((1, hps, 1, HEAD_DIM), lambda b, hg, pt, idx: (b, hg, 0, 0))
    pages = [page_spec(hh, r, pg) for hh in range(hps) for r in range(MOBA_TOPK) for pg in range(PAGES_PER_BLOCK)]
    grid_spec = pltpu.PrefetchScalarGridSpec(
        num_scalar_prefetch=2,
        grid=(nseq, N_HEADS // hps),
        in_specs=[tok, tok, tok] + pages + pages,
        out_specs=tok,
    )
    return pl.pallas_call(
        functools.partial(_sattn_kernel, past_len=past_len),
        grid_spec=grid_spec,
        out_shape=jax.ShapeDtypeStruct((nseq, N_HEADS, 1, HEAD_DIM), F32),
        compiler_params=_params(("arbitrary", "arbitrary")),
        name="moba_attn_sample",
    )(page_table, idx_flat, q4, kn4, vn4, *([cache_kt] * len(pages)), *([cache_vt] * len(pages)))


def kernel(x_prompt, x_sample, cache_k, cache_v, state_conv, page_table, c_prompt, c_sample, w_ada, b_ada, g_norm1, w_in, g_q, g_k, w_attn_proj, w_dwconv, b_dwconv, g_conv_ln, b_conv_ln, w_conv_proj, w_out, g_norm2, w_ffn_in, w_ffn_out):
    nb, seq, _ = x_prompt.shape
    ns = x_sample.shape[0]

    w_in_b = _to_bf16(w_in)
    wap_b = w_attn_proj.astype(BF16)
    wcp_b = w_conv_proj.astype(BF16)
    wo_b = w_out.astype(BF16)
    wfi_b = _to_bf16(w_ffn_in)
    wfo_b = _to_bf16(w_ffn_out)
    gq_t = jnp.tile(g_q, N_HEADS).reshape(1, ATTN_WIDTH)
    gk_t = jnp.tile(g_k, N_HEADS).reshape(1, ATTN_WIDTH)
    wdw = w_dwconv.reshape(CONV_WIDTH, CONV_CH)
    bdw = b_dwconv.reshape(1, CONV_CH)
    gln = g_conv_ln.reshape(1, CONV_CH)
    bln = b_conv_ln.reshape(1, CONV_CH)

    mod_p, mod_s = _ada(c_prompt, c_sample, w_ada, b_ada)
    mod_p3 = mod_p.reshape(nb, 1, 6 * D_MODEL)
    mod_s3 = mod_s.reshape(1, ns, 6 * D_MODEL)

    xs3 = x_sample.reshape(1, ns, D_MODEL)
    qs, ks, vs, us = _inproj(xs3, mod_s3, g_norm1, w_in_b, gq_t, gk_t, tm=ns, wq32=w_in)
    qs3 = qs.reshape(ns, 1, ATTN_WIDTH)
    cache_kt = jnp.transpose(cache_k, (0, 2, 3, 1))
    cache_vt = jnp.transpose(cache_v, (0, 2, 3, 1))

    host_a = (page_table, qs3, cache_kt, 0, IN_PROJ_HOST_CHUNKS)
    q, k, v, kb, vt, u, km, gates_a = _inproj(x_prompt, mod_p3, g_norm1, w_in_b, gq_t, gk_t, tm=256, gate_host=host_a)
    attn = _attn(q, kb, vt, km.reshape(nb, seq // MOBA_BLOCK, ATTN_WIDTH))
    wdw8 = jnp.broadcast_to(wdw[:, None, :], (CONV_WIDTH, SUBLANES, CONV_CH))
    x1 = _mix_prompt(x_prompt, attn, u, mod_p3, g_norm1, wdw8, bdw, gln, bln, w_in_b, wap_b, wcp_b, wo_b, tm=256)
    host_b = (page_table, qs3, cache_kt, gates_a.shape[0] * gates_a.shape[1], FFN_HOST_CHUNKS)
    y_prompt, gates_b = _ffn(x1, mod_p3, g_norm2, wfi_b, wfo_b, tm=256, name="ffn_prompt", gate_host=host_b)
    chunks_per_seq = page_table.shape[1] // HOST_CHUNK_PAGES
    gates = jnp.concatenate([g.reshape(-1, chunks_per_seq, N_HEADS, LANES) for g in (gates_a, gates_b)], axis=0)
    assert gates.shape[0] == ns
    k_prompt = k.reshape(nb, seq, N_HEADS, HEAD_DIM)
    v_prompt = v.reshape(nb, seq, N_HEADS, HEAD_DIM)
    conv_prompt = u[:, seq - (CONV_WIDTH - 1):, :]

    idx = _topk(gates)
    idx_flat = idx[:, :, :MOBA_TOPK].reshape(-1)
    head4 = lambda t: t.reshape(ns, N_HEADS, 1, HEAD_DIM)
    attn_s = _sattn(page_table, idx_flat, head4(qs), head4(ks), head4(vs), cache_kt, cache_vt)
    x1_s, st_new = _mix_sample(
        xs3[0], attn_s.reshape(ns, ATTN_WIDTH), us[0], jnp.transpose(state_conv, (1, 0, 2)),
        mod_s, g_norm1, wdw, bdw, gln, bln, w_in_b, wap_b, wcp_b, wo_b)
    (y_sample,) = _ffn(x1_s.reshape(1, ns, D_MODEL), mod_s3, g_norm2, wfi_b, wfo_b, tm=ns, name="ffn_sample")

    return (y_prompt, y_sample.reshape(ns, 1, D_MODEL), k_prompt, v_prompt, conv_prompt,
            ks.reshape(ns, 1, N_HEADS, HEAD_DIM), vs.reshape(ns, 1, N_HEADS, HEAD_DIM),
            jnp.transpose(st_new, (1, 0, 2)))
```

```python
import functools

import jax
import jax.numpy as jnp
from jax import lax
from jax.experimental import pallas as pl
from jax.experimental.pallas import tpu as pltpu

D_MODEL = 1024
N_HEADS = 8
HEAD_DIM = 64
ATTN_WIDTH = N_HEADS * HEAD_DIM
CONV_CH = 512
CONV_WIDTH = 31
CONV_HALO = 32
CONV_ROWS = 32
MOBA_BLOCK = 256
MOBA_TOPK = 3
PAGE_SIZE = 128
PAGES_PER_BLOCK = MOBA_BLOCK // PAGE_SIZE
N_QKVU = 3 * ATTN_WIDTH + 2 * CONV_CH
GATE_COLS = 512
D_FF = 2816
FFN_CHUNK = 256
CAST_ROWS = 128
EPS = 1e-6
NEG = -1e30
BIG = 1e30
SM_SCALE = HEAD_DIM ** -0.5
LOG2E = 1.4426950408889634

LANES = 128
SUBLANES = 8
BF16_ROWS = 16
HEADS_PER_LANE_TILE = LANES // HEAD_DIM
VMEM_LIMIT = 56 * 1024 * 1024

F32 = jnp.float32
BF16 = jnp.bfloat16
HIGHEST = lax.Precision.HIGHEST


def _sigmoid(x):
    return 0.5 * jnp.tanh(0.5 * x) + 0.5


def _silu(x):
    return x * _sigmoid(x)


def _dot(a, b, precision=None):
    return jnp.dot(a, b, precision=precision, preferred_element_type=F32)


def _dot_nt(a, b):
    return lax.dot_general(a, b, (((1,), (1,)), ((), ())), preferred_element_type=F32)


def _iota(shape, dim):
    return lax.broadcasted_iota(jnp.int32, shape, dim)


def _alibi_slope(h):
    return 2.0 ** -(h + 1)


def _mod_rms_norm(x, gain, scale, shift):
    ms = jnp.mean(x * x, axis=-1, keepdims=True)
    return x * lax.rsqrt(ms + EPS) * gain * (1.0 + scale) + shift


def _resident(shape):
    return pl.BlockSpec(shape, lambda *_: (0,) * len(shape), pipeline_mode=pl.Buffered(1))


def _weight_cols(width, col_block):
    return pl.BlockSpec((D_MODEL, width), lambda *_: (0, col_block), pipeline_mode=pl.Buffered(1))


def _params(semantics):
    return pltpu.CompilerParams(dimension_semantics=semantics, vmem_limit_bytes=VMEM_LIMIT)


def _cast_kernel(w_ref, o_ref):
    o_ref[...] = w_ref[...].astype(BF16)


def _to_bf16(w):
    rows, cols = w.shape
    spec = pl.BlockSpec((CAST_ROWS, cols), lambda i: (i, 0))
    return pl.pallas_call(
        _cast_kernel,
        grid=(rows // CAST_ROWS,),
        in_specs=[spec],
        out_specs=spec,
        out_shape=jax.ShapeDtypeStruct(w.shape, BF16),
        compiler_params=_params(("arbitrary",)),
        name="weights_to_bf16",
    )(w)


def _ada_kernel(cp_ref, cs_ref, w_ref, b_ref, mp_ref, ms_ref):
    w = w_ref[...].astype(BF16)
    b = b_ref[...]
    mp_ref[...] = _dot(_silu(cp_ref[...]).astype(BF16), w) + b
    ms_ref[...] = _dot(_silu(cs_ref[...]).astype(BF16), w) + b


def _ada(c_prompt, c_sample, w_ada, b_ada):
    nb, ns = c_prompt.shape[0], c_sample.shape[0]
    n_out = w_ada.shape[1]
    tn = D_MODEL
    return pl.pallas_call(
        _ada_kernel,
        grid=(n_out // tn,),
        in_specs=[
            pl.BlockSpec((nb, D_MODEL), lambda j: (0, 0)),
            pl.BlockSpec((ns, D_MODEL), lambda j: (0, 0)),
            pl.BlockSpec((D_MODEL, tn), lambda j: (0, j)),
            pl.BlockSpec((1, tn), lambda j: (0, j)),
        ],
        out_specs=[
            pl.BlockSpec((nb, tn), lambda j: (0, j)),
            pl.BlockSpec((ns, tn), lambda j: (0, j)),
        ],
        out_shape=[jax.ShapeDtypeStruct((nb, n_out), F32), jax.ShapeDtypeStruct((ns, n_out), F32)],
        compiler_params=_params(("arbitrary",)),
        name="ada_mod",
    )(c_prompt, c_sample, w_ada, b_ada.reshape(1, n_out))


def _head_rms_norm(t, gain, group_ones):
    sq = t * t
    hi = sq.astype(BF16)
    lo = (sq - hi.astype(F32)).astype(BF16)
    ss = _dot(hi, group_ones) + _dot(lo, group_ones)
    return t * lax.rsqrt(ss * (1.0 / HEAD_DIM) + EPS) * gain


def _inproj_kernel(*refs, host):
    if host is not None:
        pt_ref, x_ref, sh_ref, sc_ref, g1_ref, w_ref, gq_ref, gk_ref = refs[:8]
        qs_refs, kt_hbm = refs[8:8 + host.cps], refs[8 + host.cps]
        q_ref, k_ref, v_ref, kb_ref, vt_ref, u_ref, km_ref, gate_ref, page_buf, page_sem = refs[9 + host.cps:]
        slot = host.fetch(pt_ref, kt_hbm, page_buf, page_sem)
    else:
        x_ref, sh_ref, sc_ref, g1_ref, w_ref, gq_ref, gk_ref, wq32_ref, q_ref, k_ref, v_ref, u_ref = refs
    prompt = host is not None
    h = _mod_rms_norm(x_ref[0], g1_ref[...], sc_ref[0], sh_ref[0])
    hb = h.astype(BF16)

    def proj(c0, c1):
        return _dot(hb, w_ref[:, c0:c1])

    a, c = ATTN_WIDTH, CONV_CH
    row = _iota((a, a), 0)
    col = _iota((a, a), 1)
    group_ones = jnp.where((row >> 6) == (col >> 6), 1.0, 0.0).astype(BF16)

    q_pre = proj(0, a) if prompt else _dot(h, wq32_ref[...], HIGHEST)
    q = _head_rms_norm(q_pre, gq_ref[...], group_ones)
    k = _head_rms_norm(proj(a, 2 * a), gk_ref[...], group_ones)
    v = proj(2 * a, 3 * a)
    q_ref[0] = q
    k_ref[0] = k
    v_ref[0] = v
    glu_a = proj(3 * a, 3 * a + c)
    glu_b = proj(3 * a + c, 3 * a + 2 * c)
    u_ref[0] = glu_a * _sigmoid(glu_b)
    if prompt:
        for r in range(k.shape[0] // MOBA_BLOCK):
            rows = slice(r * MOBA_BLOCK, (r + 1) * MOBA_BLOCK)
            kb_ref[0, r] = k[rows].astype(BF16)
            vt_ref[0, r] = v[rows].T.astype(BF16)
            km_ref[0, r] = jnp.mean(k[rows], axis=0, keepdims=True)
        host.reduce(qs_refs, page_buf.at[slot], gate_ref)


def _inproj(x3, mod3, g_norm1, w_in_b, gq_t, gk_t, *, tm, wq32=None, gate_host=None):
    g, s, _ = x3.shape
    r = mod3.shape[1]
    nt = s // tm
    prompt = gate_host is not None
    host = _GateHost(*gate_host, grid=(g, nt)) if prompt else None

    def mod_spec(chunk):
        if r == 1:
            return pl.BlockSpec((1, 1, D_MODEL), lambda b, t, *_: (b, 0, chunk))
        return pl.BlockSpec((1, tm, D_MODEL), lambda b, t, *_: (b, t, chunk))

    tok = lambda width: pl.BlockSpec((1, tm, width), lambda b, t, *_: (b, t, 0))
    in_specs = [tok(D_MODEL), mod_spec(0), mod_spec(1),
                _resident((1, D_MODEL)), _weight_cols(N_QKVU, 0), _resident((1, ATTN_WIDTH)), _resident((1, ATTN_WIDTH))]
    operands = [x3, mod3, mod3, g_norm1.reshape(1, D_MODEL), w_in_b, gq_t, gk_t]
    out_specs = [tok(ATTN_WIDTH), tok(ATTN_WIDTH), tok(ATTN_WIDTH)]
    out_shape = [jax.ShapeDtypeStruct((g, s, ATTN_WIDTH), F32)] * 3
    if prompt:
        nblk = tm // MOBA_BLOCK
        blk_spec = lambda rows, cols: pl.BlockSpec((1, nblk, rows, cols), lambda b, t, *_: (b, t, 0, 0))
        in_specs += host.in_specs()
        operands += host.operands()
        out_specs += [blk_spec(MOBA_BLOCK, ATTN_WIDTH), blk_spec(ATTN_WIDTH, MOBA_BLOCK), tok(CONV_CH),
                      blk_spec(1, ATTN_WIDTH), host.out_spec()]
        out_shape += [jax.ShapeDtypeStruct((g, s // MOBA_BLOCK, MOBA_BLOCK, ATTN_WIDTH), BF16),
                      jax.ShapeDtypeStruct((g, s // MOBA_BLOCK, ATTN_WIDTH, MOBA_BLOCK), BF16),
                      jax.ShapeDtypeStruct((g, s, CONV_CH), F32),
                      jax.ShapeDtypeStruct((g, s // MOBA_BLOCK, 1, ATTN_WIDTH), F32),
                      host.out_shape()]
        prefetch = [host.page_table]
    else:
        in_specs += [_weight_cols(ATTN_WIDTH, 0)]
        operands += [wq32]
        out_specs += [tok(CONV_CH)]
        out_shape += [jax.ShapeDtypeStruct((g, s, CONV_CH), F32)]
        prefetch = []
    grid_spec = pltpu.PrefetchScalarGridSpec(
        num_scalar_prefetch=len(prefetch), grid=(g, nt), in_specs=in_specs, out_specs=out_specs,
        scratch_shapes=host.scratch_shapes() if prompt else [])
    return pl.pallas_call(
        functools.partial(_inproj_kernel, host=host),
        grid_spec=grid_spec,
        out_shape=out_shape,
        compiler_params=_params(("arbitrary", "arbitrary")),
        name="in_proj_prompt" if prompt else "in_proj_sample",
    )(*prefetch, *operands)


def _moba_keep_t(q_t, km, q_blk, keep_ref):
    nb = km.shape[0]
    assert nb == SUBLANES
    kmb = jnp.concatenate([km] * N_HEADS, axis=0)
    row = _iota(kmb.shape, 0)
    col = _iota(kmb.shape, 1)
    kmb = jnp.where((row >> 3) == (col >> 6), kmb, 0.0)
    gate_t = _dot(kmb, q_t, HIGHEST)
    n_idx = _iota((nb, q_t.shape[1]), 0)
    for h in range(N_HEADS):
        gate = gate_t[h * nb:(h + 1) * nb]
        beaten_by = jnp.zeros(gate.shape, jnp.int32)
        for s in range(1, nb):
            g_m = pltpu.roll(gate, s, axis=0)
            lower = n_idx >= s
            m_idx = jnp.where(lower, n_idx - s, n_idx - s + nb)
            beats = (g_m > gate) | (lower & (g_m == gate))
            beaten_by += jnp.where(beats & (m_idx < q_blk), 1, 0)
        keep = (n_idx < q_blk) & (beaten_by < MOBA_TOPK)
        keep_ref[h * nb:(h + 1) * nb, :] = jnp.where(keep, 1.0, 0.0)


def _attn_kernel(q_ref, kb_ref, vt_ref, km_ref, o_ref, qz_ref, keep_ref, bias_ref, m_ref, accl_ref, x_ref, p_ref):
    i = pl.program_id(1)
    nb = km_ref.shape[1]
    blk = MOBA_BLOCK
    q_t = q_ref[0].T
    _moba_keep_t(q_t, km_ref[0], i, keep_ref)

    key_r = _iota((blk, blk), 0)
    qry_c = _iota((blk, blk), 1)
    rel_t = (qry_c - key_r).astype(F32)
    causal = qry_c >= key_r
    qz_t = q_t * (SM_SCALE * LOG2E)
    pair_row = _iota((LANES, blk), 0)
    ones_rows = jnp.ones((BF16_ROWS, blk), BF16)
    for h in range(N_HEADS):
        hp, half = divmod(h, HEADS_PER_LANE_TILE)
        bias_ref[h] = rel_t * (-_alibi_slope(h) * LOG2E)
        qz_ref[h] = jnp.where((pair_row >> 6) == half, qz_t[hp * LANES:(hp + 1) * LANES], 0.0).astype(BF16)

    def scores(jb):
        for h in range(N_HEADS):
            hp = h // HEADS_PER_LANE_TILE
            x_ref[h] = _dot(kb_ref[0, jb, :, hp * LANES:(hp + 1) * LANES], qz_ref[h]) + bias_ref[h]

    def values(h, jb):
        return jnp.concatenate([vt_ref[0, jb, h * HEAD_DIM:(h + 1) * HEAD_DIM, :], ones_rows], axis=0)

    scores(i)
    for h in range(N_HEADS):
        x = jnp.where(causal, x_ref[h], NEG)
        m = jnp.max(x, axis=0, keepdims=True)
        p_ref[h] = jnp.exp2(x - m).astype(BF16)
        m_ref[h:h + 1, :] = m
    for h in range(N_HEADS):
        accl_ref[h] = _dot(values(h, i), p_ref[h])

    def past_block(jb, carry):
        offset = ((i - jb) * blk).astype(F32)
        scores(jb)
        alphas = []
        for h in range(N_HEADS):
            x = x_ref[h]
            c_j = offset * (_alibi_slope(h) * LOG2E)
            kept = keep_ref[pl.ds(h * nb + jb, 1), :] > 0.5
            m_old = m_ref[h:h + 1, :]
            m_new = jnp.maximum(m_old, jnp.where(kept, jnp.max(x, axis=0, keepdims=True) - c_j, NEG))
            alphas.append(jnp.exp2(m_old - m_new))
            p_ref[h] = jnp.exp2(x - jnp.where(kept, m_new + c_j, BIG)).astype(BF16)
            m_ref[h:h + 1, :] = m_new
        for h in range(N_HEADS):
            accl_ref[h] = alphas[h] * accl_ref[h] + _dot(values(h, jb), p_ref[h])
        return carry

    lax.fori_loop(0, i, past_block, 0)

    outs = [accl_ref[h, 0:HEAD_DIM, :] / accl_ref[h, HEAD_DIM:HEAD_DIM + 1, :] for h in range(N_HEADS)]
    o_ref[0] = jnp.concatenate(outs, axis=0).T.astype(BF16)


def _attn(q, kb, vt, km):
    b, s, _ = q.shape
    nq = s // MOBA_BLOCK
    return pl.pallas_call(
        _attn_kernel,
        grid=(b, nq),
        in_specs=[
            pl.BlockSpec((1, MOBA_BLOCK, ATTN_WIDTH), lambda bi, i: (bi, i, 0)),
            pl.BlockSpec((1, nq, MOBA_BLOCK, ATTN_WIDTH), lambda bi, i: (bi, 0, 0, 0)),
            pl.BlockSpec((1, nq, ATTN_WIDTH, MOBA_BLOCK), lambda bi, i: (bi, 0, 0, 0)),
            pl.BlockSpec((1, nq, ATTN_WIDTH), lambda bi, i: (bi, 0, 0)),
        ],
        out_specs=pl.BlockSpec((1, MOBA_BLOCK, ATTN_WIDTH), lambda bi, i: (bi, i, 0)),
        out_shape=jax.ShapeDtypeStruct((b, s, ATTN_WIDTH), BF16),
        scratch_shapes=[
            pltpu.VMEM((N_HEADS, LANES, MOBA_BLOCK), BF16),
            pltpu.VMEM((N_HEADS * nq, MOBA_BLOCK), F32),
            pltpu.VMEM((N_HEADS, MOBA_BLOCK, MOBA_BLOCK), F32),
            pltpu.VMEM((N_HEADS, MOBA_BLOCK), F32),
            pltpu.VMEM((N_HEADS, HEAD_DIM + BF16_ROWS, MOBA_BLOCK), F32),
            pltpu.VMEM((N_HEADS, MOBA_BLOCK, MOBA_BLOCK), F32),
            pltpu.VMEM((N_HEADS, MOBA_BLOCK, MOBA_BLOCK), BF16),
        ],
        compiler_params=_params(("arbitrary", "arbitrary")),
        name="moba_attn_prompt",
    )(q, kb, vt, km)


def _gate_weight_specs():
    first = N_QKVU // GATE_COLS
    return [_weight_cols(GATE_COLS, first + j) for j in range(2 * D_MODEL // GATE_COLS)]


def _mix_head(attn_b, x, sh1, sc1, g1, wg_refs, wap_ref):
    hb = _mod_rms_norm(x, g1, sc1, sh1).astype(BF16)
    gates = [_dot(hb, w_ref[...]) for w_ref in wg_refs]
    per_gate = len(gates) // 2
    sga = _sigmoid(jnp.concatenate(gates[:per_gate], axis=1))
    sgc = _sigmoid(jnp.concatenate(gates[per_gate:], axis=1))
    return sga * _dot(attn_b, wap_ref[...]), sgc


def _mix_tail(cv, gated_attn, sgc, x, gt1, gln, bln, wcp_ref, wo_ref):
    mu = jnp.mean(cv, axis=-1, keepdims=True)
    cen = cv - mu
    var = jnp.mean(cen * cen, axis=-1, keepdims=True)
    y = cen * lax.rsqrt(var + EPS) * gln + bln
    cvp = _dot(_silu(y).astype(BF16), wcp_ref[...])
    mix = _dot((gated_attn + sgc * cvp).astype(BF16), wo_ref[...])
    return x + gt1 * mix


def _mix_prompt_kernel(x_ref, attn_ref, u_ref, halo_ref, sh1_ref, sc1_ref, gt1_ref, g1_ref, wdw_ref, bdw_ref,
                       gln_ref, bln_ref, wg0_ref, wg1_ref, wg2_ref, wg3_ref, wap_ref, wcp_ref, wo_ref,
                       x1_ref, ext_ref, cv_ref):
    t = pl.program_id(1)
    tm = u_ref.shape[1]
    x = x_ref[0]
    gated_attn, sgc = _mix_head(attn_ref[0], x, sh1_ref[0], sc1_ref[0], g1_ref[...],
                                (wg0_ref, wg1_ref, wg2_ref, wg3_ref), wap_ref)
    ext_ref[0, 0:CONV_HALO, :] = jnp.where(t > 0, halo_ref[0], 0.0)
    ext_ref[0, CONV_HALO:, :] = u_ref[0]
    shifted_rows = tm + CONV_HALO - SUBLANES
    for r in range(1, SUBLANES):
        ext_ref[r, 0:shifted_rows, :] = ext_ref[0, r:r + shifted_rows, :]
    lead = CONV_HALO - (CONV_WIDTH - 1)
    for r0 in range(0, tm, CONV_ROWS):
        acc = jnp.zeros((CONV_ROWS // SUBLANES, SUBLANES, CONV_CH), F32)
        for w in range(CONV_WIDTH):
            r = (lead + w) % SUBLANES
            a = lead + w - r + r0
            acc = acc + ext_ref[r, a:a + CONV_ROWS, :].reshape(acc.shape) * wdw_ref[w]
        cv_ref[r0:r0 + CONV_ROWS, :] = acc.reshape(CONV_ROWS, CONV_CH) + bdw_ref[...]
    x1_ref[0] = _mix_tail(cv_ref[...], gated_attn, sgc, x, gt1_ref[0], gln_ref[...], bln_ref[...], wcp_ref, wo_ref)


def _mix_prompt(x3, attn, u, mod3, g_norm1, wdw, bdw, gln, bln, w_in_b, wap_b, wcp_b, wo_b, *, tm):
    b, s, _ = x3.shape
    nt = s // tm
    halo_per_tile = tm // CONV_HALO
    tok = lambda width: pl.BlockSpec((1, tm, width), lambda bi, t: (bi, t, 0))
    full = lambda shape: pl.BlockSpec(shape, lambda bi, t: (0,) * len(shape))
    mod_spec = lambda chunk: pl.BlockSpec((1, 1, D_MODEL), lambda bi, t: (bi, 0, chunk))
    return pl.pallas_call(
        _mix_prompt_kernel,
        grid=(b, nt),
        in_specs=[
            tok(D_MODEL), tok(ATTN_WIDTH), tok(CONV_CH),
            pl.BlockSpec((1, CONV_HALO, CONV_CH), lambda bi, t: (bi, jnp.maximum(t * halo_per_tile - 1, 0), 0)),
            mod_spec(0), mod_spec(1), mod_spec(2),
            full((1, D_MODEL)),
            full(wdw.shape), full((1, CONV_CH)), full((1, CONV_CH)), full((1, CONV_CH)),
            *_gate_weight_specs(), full(wap_b.shape), full(wcp_b.shape), full(wo_b.shape),
        ],
        out_specs=tok(D_MODEL),
        out_shape=jax.ShapeDtypeStruct((b, s, D_MODEL), F32),
        scratch_shapes=[pltpu.VMEM((SUBLANES, CONV_HALO + tm, CONV_CH), F32), pltpu.VMEM((tm, CONV_CH), F32)],
        compiler_params=_params(("arbitrary", "arbitrary")),
        name="mix_prompt",
    )(x3, attn, u, u, mod3, mod3, mod3, g_norm1.reshape(1, D_MODEL), wdw, bdw, gln, bln,
      *([w_in_b] * len(_gate_weight_specs())), wap_b, wcp_b, wo_b)


def _mix_sample_kernel(x_ref, attn_ref, u_ref, st_ref, sh1_ref, sc1_ref, gt1_ref, g1_ref, wdw_ref, bdw_ref,
                       gln_ref, bln_ref, wg0_ref, wg1_ref, wg2_ref, wg3_ref, wap_ref, wcp_ref, wo_ref,
                       x1_ref, st_out_ref):
    u = u_ref[...]
    past = CONV_WIDTH - 1
    cv = u * wdw_ref[past:past + 1, :] + bdw_ref[...]
    for w in range(past):
        cv = cv + st_ref[w] * wdw_ref[w:w + 1, :]
    for w in range(1, past):
        st_out_ref[w - 1] = st_ref[w]
    st_out_ref[past - 1] = u
    x = x_ref[...]
    gated_attn, sgc = _mix_head(attn_ref[...].astype(BF16), x, sh1_ref[...], sc1_ref[...], g1_ref[...],
                                (wg0_ref, wg1_ref, wg2_ref, wg3_ref), wap_ref)
    x1_ref[...] = _mix_tail(cv, gated_attn, sgc, x, gt1_ref[...], gln_ref[...], bln_ref[...], wcp_ref, wo_ref)


def _mix_sample(x2, attn, u, st_t, mod_s, g_norm1, wdw, bdw, gln, bln, w_in_b, wap_b, wcp_b, wo_b):
    n = x2.shape[0]
    full = lambda shape: pl.BlockSpec(shape, lambda j: (0,) * len(shape))
    mod_spec = lambda chunk: pl.BlockSpec((n, D_MODEL), lambda j: (0, chunk))
    return pl.pallas_call(
        _mix_sample_kernel,
        grid=(1,),
        in_specs=[
            full(x2.shape), full(attn.shape), full(u.shape), full(st_t.shape),
            mod_spec(0), mod_spec(1), mod_spec(2),
            full((1, D_MODEL)),
            full(wdw.shape), full((1, CONV_CH)), full((1, CONV_CH)), full((1, CONV_CH)),
            *_gate_weight_specs(), full(wap_b.shape), full(wcp_b.shape), full(wo_b.shape),
        ],
        out_specs=[full(x2.shape), full(st_t.shape)],
        out_shape=[jax.ShapeDtypeStruct(x2.shape, F32), jax.ShapeDtypeStruct(st_t.shape, F32)],
        compiler_params=_params(("arbitrary",)),
        name="mix_sample",
    )(x2, attn, u, st_t, mod_s, mod_s, mod_s, g_norm1.reshape(1, D_MODEL), wdw, bdw, gln, bln,
      *([w_in_b] * len(_gate_weight_specs())), wap_b, wcp_b, wo_b)


def _ffn_kernel(*refs, host):
    if host is not None:
        pt_ref, refs = refs[0], refs[1:]
        qs_refs, kt_hbm = refs[7:7 + host.cps], refs[7 + host.cps]
        y_ref, gate_ref, page_buf, page_sem = refs[8 + host.cps:]
        slot = host.fetch(pt_ref, kt_hbm, page_buf, page_sem)
    else:
        y_ref = refs[7]
    x_ref, sh_ref, sc_ref, gt_ref, g2_ref, wi_ref, wo_ref = refs[:7]
    x = x_ref[0]
    hb = _mod_rms_norm(x, g2_ref[...], sc_ref[0], sh_ref[0]).astype(BF16)
    acc = jnp.zeros(x.shape, F32)
    for c in range(D_FF // FFN_CHUNK):
        c0 = c * FFN_CHUNK
        f_g = _dot(hb, wi_ref[:, c0:c0 + FFN_CHUNK])
        f_u = _dot(hb, wi_ref[:, D_FF + c0:D_FF + c0 + FFN_CHUNK])
        acc = acc + _dot((_silu(f_g) * f_u).astype(BF16), wo_ref[c0:c0 + FFN_CHUNK, :])
    y_ref[0] = x + gt_ref[0] * acc
    if host is not None:
        host.reduce(qs_refs, page_buf.at[slot], gate_ref)


def _ffn(x3, mod3, g_norm2, wi_b, wo_b, *, tm, name, gate_host=None):
    g, s, _ = x3.shape
    r = mod3.shape[1]
    nt = s // tm
    host = _GateHost(*gate_host, grid=(g, nt)) if gate_host is not None else None

    def mod_spec(chunk):
        if r == 1:
            return pl.BlockSpec((1, 1, D_MODEL), lambda b, t, *_: (b, 0, chunk))
        return pl.BlockSpec((1, tm, D_MODEL), lambda b, t, *_: (b, t, chunk))

    tok = pl.BlockSpec((1, tm, D_MODEL), lambda b, t, *_: (b, t, 0))
    in_specs = [tok, mod_spec(3), mod_spec(4), mod_spec(5),
                _resident((1, D_MODEL)), _resident(wi_b.shape), _resident(wo_b.shape)]
    operands = [x3, mod3, mod3, mod3, g_norm2.reshape(1, D_MODEL), wi_b, wo_b]
    out_specs = [tok]
    out_shape = [jax.ShapeDtypeStruct(x3.shape, F32)]
    prefetch = []
    if host is not None:
        in_specs += host.in_specs()
        operands += host.operands()
        out_specs += [host.out_spec()]
        out_shape += [host.out_shape()]
        prefetch = [host.page_table]
    grid_spec = pltpu.PrefetchScalarGridSpec(
        num_scalar_prefetch=len(prefetch), grid=(g, nt), in_specs=in_specs, out_specs=out_specs,
        scratch_shapes=host.scratch_shapes() if host is not None else [])
    return pl.pallas_call(
        functools.partial(_ffn_kernel, host=host),
        grid_spec=grid_spec,
        out_shape=out_shape,
        compiler_params=_params(("arbitrary", "arbitrary")),
        name=name,
    )(*prefetch, *operands)


HOST_CHUNK_PAGES = 16
HOST_BLOCKS = HOST_CHUNK_PAGES // PAGES_PER_BLOCK
IN_PROJ_HOST_CHUNKS = 1
FFN_HOST_CHUNKS = 3


def _chunk_gate(qs_ref, pages_ref, first_page, chunk_in_seq):
    qcol = jnp.broadcast_to(qs_ref[0], (LANES, ATTN_WIDTH)).T
    sub = _iota((N_HEADS, LANES), 0)
    lane = _iota((N_HEADS, LANES), 1)
    gate = jnp.zeros((N_HEADS, LANES), F32)
    for h in range(N_HEADS):
        q_h = qcol[h * HEAD_DIM:(h + 1) * HEAD_DIM]
        col = jnp.zeros((SUBLANES, LANES), F32)
        for blk in range(HOST_BLOCKS):
            kt = pages_ref[first_page + blk * PAGES_PER_BLOCK, h]
            for pg in range(1, PAGES_PER_BLOCK):
                kt = kt + pages_ref[first_page + blk * PAGES_PER_BLOCK + pg, h]
            part = jnp.sum((kt * q_h).reshape(HEAD_DIM // SUBLANES, SUBLANES, PAGE_SIZE), axis=0)
            col = jnp.where(lane == chunk_in_seq * HOST_BLOCKS + blk, jnp.sum(part, axis=1, keepdims=True), col)
        row = jnp.sum(col, axis=0, keepdims=True) * (1.0 / MOBA_BLOCK)
        gate = jnp.where(sub == h, row, gate)
    return gate


class _GateHost:
    def __init__(self, page_table, q_s3, cache_kt, first_chunk, chunks_per_step, grid):
        self.chunks_per_seq = page_table.shape[1] // HOST_CHUNK_PAGES
        self.cps = chunks_per_step
        self.steps = grid[0] * grid[1]
        self.nt = grid[1]
        assert first_chunk + self.steps * self.cps <= page_table.shape[0] * self.chunks_per_seq
        self.page_table, self.q_s3, self.cache_kt, self.first_chunk = page_table, q_s3, cache_kt, first_chunk

    def _chunk(self, step, j):
        return self.first_chunk + step * self.cps + j

    def in_specs(self):
        def q_spec(j):
            return pl.BlockSpec((1, 1, ATTN_WIDTH),
                                lambda b, t, pt: (self._chunk(b * self.nt + t, j) // self.chunks_per_seq, 0, 0))
        return [q_spec(j) for j in range(self.cps)] + [pl.BlockSpec(memory_space=pl.ANY)]

    def operands(self):
        return [self.q_s3] * self.cps + [self.cache_kt]

    def scratch_shapes(self):
        pages = self.cps * HOST_CHUNK_PAGES
        return [pltpu.VMEM((2, pages) + self.cache_kt.shape[1:], F32), pltpu.SemaphoreType.DMA((2,))]

    def _copies(self, pt_ref, kt_hbm, page_buf, page_sem, step, slot, for_wait):
        copies = []
        for j in range(self.cps):
            chunk = self._chunk(step, j)
            seq = chunk // self.chunks_per_seq
            first = (chunk % self.chunks_per_seq) * HOST_CHUNK_PAGES
            for r in range(HOST_CHUNK_PAGES):
                src = kt_hbm.at[0 if for_wait else pt_ref[seq, first + r]]
                dst = page_buf.at[slot, j * HOST_CHUNK_PAGES + r]
                copies.append(pltpu.make_async_copy(src, dst, page_sem.at[slot]))
        return copies

    def fetch(self, pt_ref, kt_hbm, page_buf, page_sem):
        step = pl.program_id(0) * self.nt + pl.program_id(1)
        slot = step % 2

        @pl.when(step == 0)
        def _():
            for c in self._copies(pt_ref, kt_hbm, page_buf, page_sem, step, slot, False):
                c.start()

        @pl.when(step + 1 < self.steps)
        def _():
            for c in self._copies(pt_ref, kt_hbm, page_buf, page_sem, step + 1, 1 - slot, False):
                c.start()

        for c in self._copies(pt_ref, kt_hbm, page_buf, page_sem, step, slot, True):
            c.wait()
        return slot

    def reduce(self, qs_refs, pages_ref, gate_ref):
        step = pl.program_id(0) * self.nt + pl.program_id(1)
        for j in range(self.cps):
            chunk_in_seq = self._chunk(step, j) % self.chunks_per_seq
            gate_ref[0, j] = _chunk_gate(qs_refs[j], pages_ref, j * HOST_CHUNK_PAGES, chunk_in_seq)

    def out_spec(self):
        return pl.BlockSpec((1, self.cps, N_HEADS, LANES), lambda b, t, pt: (b * self.nt + t, 0, 0, 0))

    def out_shape(self):
        return jax.ShapeDtypeStruct((self.steps, self.cps, N_HEADS, LANES), F32)


def _topk_kernel(gp_ref, idx_ref):
    chunks = gp_ref.shape[1]
    g = gp_ref[:, 0]
    for c in range(1, chunks):
        g = g + gp_ref[:, c]
    lane = _iota(g.shape, 2)
    g = jnp.where(lane < chunks * HOST_BLOCKS, g, -jnp.inf)
    out = jnp.zeros(g.shape, jnp.int32)
    for r in range(MOBA_TOPK):
        best = jnp.max(g, axis=2, keepdims=True)
        pick = jnp.min(jnp.where(g == best, lane, LANES), axis=2, keepdims=True)
        out = jnp.where(lane == r, pick, out)
        g = jnp.where(lane == pick, -jnp.inf, g)
    idx_ref[...] = out


def _topk(gate_parts):
    nseq = gate_parts.shape[0]
    out_shape = (nseq, N_HEADS, LANES)
    return pl.pallas_call(
        _topk_kernel,
        grid=(1,),
        in_specs=[pl.BlockSpec(gate_parts.shape, lambda j: (0, 0, 0, 0))],
        out_specs=pl.BlockSpec(out_shape, lambda j: (0, 0, 0)),
        out_shape=jax.ShapeDtypeStruct(out_shape, jnp.int32),
        compiler_params=_params(("arbitrary",)),
        name="moba_topk_sample",
    )(gate_parts)


SEL_PAGES = MOBA_TOPK * PAGES_PER_BLOCK
SATTN_HEADS_PER_STEP = 4


def _sattn_kernel(pt_ref, idx_ref, q_ref, kn_ref, vn_ref, *refs, past_len):
    del pt_ref
    n_pg = SATTN_HEADS_PER_STEP * SEL_PAGES
    k_refs = refs[:n_pg]
    v_refs = refs[n_pg:2 * n_pg]
    o_ref = refs[2 * n_pg]
    b = pl.program_id(0)
    hg = pl.program_id(1)
    lane = _iota((1, PAGE_SIZE), 1)
    eye = _iota((HEAD_DIM, HEAD_DIM), 0) == _iota((HEAD_DIM, HEAD_DIM), 1)
    for hh in range(SATTN_HEADS_PER_STEP):
        h = hg * SATTN_HEADS_PER_STEP + hh
        q_row = q_ref[0, hh] * SM_SCALE
        q = jnp.sum(jnp.where(eye, q_row, 0.0), axis=1, keepdims=True)
        k_cat = jnp.concatenate([k_refs[hh * SEL_PAGES + j][0, 0] for j in range(SEL_PAGES)], axis=1)
        v_cat = jnp.concatenate([v_refs[hh * SEL_PAGES + j][0, 0] for j in range(SEL_PAGES)], axis=1)
        pos = []
        for r in range(MOBA_TOPK):
            blk = idx_ref[(b * N_HEADS + h) * MOBA_TOPK + r]
            pos += [blk * MOBA_BLOCK + pg * PAGE_SIZE + lane for pg in range(PAGES_PER_BLOCK)]
        dist = (past_len - jnp.concatenate(pos, axis=1)).astype(F32)
        slope = lax.bitcast_convert_type(jnp.full(dist.shape, 126, jnp.int32) - h << 23, F32)
        s = jnp.sum(k_cat * q, axis=0, keepdims=True) - slope * dist
        s_own = jnp.sum(q_row * kn_ref[0, hh], axis=1, keepdims=True)
        m = jnp.maximum(jnp.max(s, axis=1, keepdims=True), s_own)
        p = jnp.exp(s - m)
        p_own = jnp.exp(s_own - m)
        l = jnp.sum(p, axis=1, keepdims=True) + p_own
        acc = jnp.sum(v_cat * p, axis=1, keepdims=True)
        acc_row = jnp.sum(jnp.where(eye, acc, 0.0), axis=0, keepdims=True)
        o_ref[0, hh] = (acc_row + p_own * vn_ref[0, hh]) / l


def _sattn(page_table, idx_flat, q4, kn4, vn4, cache_kt, cache_vt):
    nseq, n_pages = page_table.shape
    past_len = n_pages * PAGE_SIZE
    hps = SATTN_HEADS_PER_STEP

    def page_spec(hh, r, pg):
        def index_map(b, hg, pt, idx):
            h = hg * hps + hh
            return (pt[b, idx[(b * N_HEADS + h) * MOBA_TOPK + r] * PAGES_PER_BLOCK + pg], h, 0, 0)
        return pl.BlockSpec((1, 1, HEAD_DIM, PAGE_SIZE), index_map)

    tok = pl.BlockSpec((1, hps, 1, HEAD_DIM), lambda b, hg, pt, idx: (b, hg, 0, 0))
    pages = [page_spec(hh, r, pg) for hh in range(hps) for r in range(MOBA_TOPK) for pg in range(PAGES_PER_BLOCK)]
    grid_spec = pltpu.PrefetchScalarGridSpec(
        num_scalar_prefetch=2,
        grid=(nseq, N_HEADS // hps),
        in_specs=[tok, tok, tok] + pages + pages,
        out_specs=tok,
    )
    return pl.pallas_call(
        functools.partial(_sattn_kernel, past_len=past_len),
        grid_spec=grid_spec,
        out_shape=jax.ShapeDtypeStruct((nseq, N_HEADS, 1, HEAD_DIM), F32),
        compiler_params=_params(("arbitrary", "arbitrary")),
        name="moba_attn_sample",
    )(page_table, idx_flat, q4, kn4, vn4, *([cache_kt] * len(pages)), *([cache_vt] * len(pages)))


def kernel(x_prompt, x_sample, cache_k, cache_v, state_conv, page_table, c_prompt, c_sample, w_ada, b_ada, g_norm1, w_in, g_q, g_k, w_attn_proj, w_dwconv, b_dwconv, g_conv_ln, b_conv_ln, w_conv_proj, w_out, g_norm2, w_ffn_in, w_ffn_out):
    nb, seq, _ = x_prompt.shape
    ns = x_sample.shape[0]

    w_in_b = _to_bf16(w_in)
    wap_b = w_attn_proj.astype(BF16)
    wcp_b = w_conv_proj.astype(BF16)
    wo_b = w_out.astype(BF16)
    wfi_b = _to_bf16(w_ffn_in)
    wfo_b = _to_bf16(w_ffn_out)
    gq_t = jnp.tile(g_q, N_HEADS).reshape(1, ATTN_WIDTH)
    gk_t = jnp.tile(g_k, N_HEADS).reshape(1, ATTN_WIDTH)
    wdw = w_dwconv.reshape(CONV_WIDTH, CONV_CH)
    bdw = b_dwconv.reshape(1, CONV_CH)
    gln = g_conv_ln.reshape(1, CONV_CH)
    bln = b_conv_ln.reshape(1, CONV_CH)

    mod_p, mod_s = _ada(c_prompt, c_sample, w_ada, b_ada)
    mod_p3 = mod_p.reshape(nb, 1, 6 * D_MODEL)
    mod_s3 = mod_s.reshape(1, ns, 6 * D_MODEL)

    xs3 = x_sample.reshape(1, ns, D_MODEL)
    qs, ks, vs, us = _inproj(xs3, mod_s3, g_norm1, w_in_b, gq_t, gk_t, tm=ns, wq32=w_in)
    qs3 = qs.reshape(ns, 1, ATTN_WIDTH)
    cache_kt = jnp.transpose(cache_k, (0, 2, 3, 1))
    cache_vt = jnp.transpose(cache_v, (0, 2, 3, 1))

    host_a = (page_table, qs3, cache_kt, 0, IN_PROJ_HOST_CHUNKS)
    q, k, v, kb, vt, u, km, gates_a = _inproj(x_prompt, mod_p3, g_norm1, w_in_b, gq_t, gk_t, tm=256, gate_host=host_a)
    attn = _attn(q, kb, vt, km.reshape(nb, seq // MOBA_BLOCK, ATTN_WIDTH))
    wdw8 = jnp.broadcast_to(wdw[:, None, :], (CONV_WIDTH, SUBLANES, CONV_CH))
    x1 = _mix_prompt(x_prompt, attn, u, mod_p3, g_norm1, wdw8, bdw, gln, bln, w_in_b, wap_b, wcp_b, wo_b, tm=256)
    host_b = (page_table, qs3, cache_kt, gates_a.shape[0] * gates_a.shape[1], FFN_HOST_CHUNKS)
    y_prompt, gates_b = _ffn(x1, mod_p3, g_norm2, wfi_b, wfo_b, tm=256, name="ffn_prompt", gate_host=host_b)
    chunks_per_seq = page_table.shape[1] // HOST_CHUNK_PAGES
    gates = jnp.concatenate([g.reshape(-1, chunks_per_seq, N_HEADS, LANES) for g in (gates_a, gates_b)], axis=0)
    assert gates.shape[0] == ns
    k_prompt = k.reshape(nb, seq, N_HEADS, HEAD_DIM)
    v_prompt = v.reshape(nb, seq, N_HEADS, HEAD_DIM)
    conv_prompt = u[:, seq - (CONV_WIDTH - 1):, :]

    idx = _topk(gates)
    idx_flat = idx[:, :, :MOBA_TOPK].reshape(-1)
    head4 = lambda t: t.reshape(ns, N_HEADS, 1, HEAD_DIM)
    attn_s = _sattn(page_table, idx_flat, head4(qs), head4(ks), head4(vs), cache_kt, cache_vt)
    x1_s, st_new = _mix_sample(
        xs3[0], attn_s.reshape(ns, ATTN_WIDTH), us[0], jnp.transpose(state_conv, (1, 0, 2)),
        mod_s, g_norm1, wdw, bdw, gln, bln, w_in_b, wap_b, wcp_b, wo_b)
    (y_sample,) = _ffn(x1_s.reshape(1, ns, D_MODEL), mod_s3, g_norm2, wfi_b, wfo_b, tm=ns, name="ffn_sample")

    return (y_prompt, y_sample.reshape(ns, 1, D_MODEL), k_prompt, v_prompt, conv_prompt,
            ks.reshape(ns, 1, N_HEADS, HEAD_DIM), vs.reshape(ns, 1, N_HEADS, HEAD_DIM),
            jnp.transpose(st_new, (1, 0, 2)))
```

```python
import functools

import jax
import jax.numpy as jnp
from jax import lax
from jax.experimental import pallas as pl
from jax.experimental.pallas import tpu as pltpu

D_MODEL = 1024
N_HEADS = 8
HEAD_DIM = 64
ATTN_WIDTH = N_HEADS * HEAD_DIM
CONV_CH = 512
CONV_WIDTH = 31
CONV_HALO = 32
CONV_ROWS = 32
MOBA_BLOCK = 256
MOBA_TOPK = 3
PAGE_SIZE = 128
PAGES_PER_BLOCK = MOBA_BLOCK // PAGE_SIZE
N_QKVU = 3 * ATTN_WIDTH + 2 * CONV_CH
GATE_COLS = 512
D_FF = 2816
FFN_CHUNK = 256
CAST_ROWS = 256
EPS = 1e-6
NEG = -1e30
BIG = 1e30
SM_SCALE = HEAD_DIM ** -0.5
LOG2E = 1.4426950408889634

LANES = 128
SUBLANES = 8
BF16_ROWS = 16
HEADS_PER_LANE_TILE = LANES // HEAD_DIM
VMEM_LIMIT = 56 * 1024 * 1024

F32 = jnp.float32
BF16 = jnp.bfloat16
HIGHEST = lax.Precision.HIGHEST


def _sigmoid(x):
    return 0.5 * jnp.tanh(0.5 * x) + 0.5


def _silu(x):
    return x * _sigmoid(x)


def _dot(a, b, precision=None):
    return jnp.dot(a, b, precision=precision, preferred_element_type=F32)


def _dot_nt(a, b):
    return lax.dot_general(a, b, (((1,), (1,)), ((), ())), preferred_element_type=F32)


def _iota(shape, dim):
    return lax.broadcasted_iota(jnp.int32, shape, dim)


def _alibi_slope(h):
    return 2.0 ** -(h + 1)


def _mod_rms_norm(x, gain, scale, shift):
    ms = jnp.mean(x * x, axis=-1, keepdims=True)
    return x * lax.rsqrt(ms + EPS) * gain * (1.0 + scale) + shift


def _resident(shape):
    return pl.BlockSpec(shape, lambda *_: (0,) * len(shape), pipeline_mode=pl.Buffered(1))


def _weight_cols(width, col_block):
    return pl.BlockSpec((D_MODEL, width), lambda *_: (0, col_block), pipeline_mode=pl.Buffered(1))


def _params(semantics):
    return pltpu.CompilerParams(dimension_semantics=semantics, vmem_limit_bytes=VMEM_LIMIT)


def _cast_kernel(w_ref, o_ref):
    o_ref[...] = w_ref[...].astype(BF16)


def _to_bf16(w):
    rows, cols = w.shape
    spec = pl.BlockSpec((CAST_ROWS, cols), lambda i: (i, 0))
    return pl.pallas_call(
        _cast_kernel,
        grid=(rows // CAST_ROWS,),
        in_specs=[spec],
        out_specs=spec,
        out_shape=jax.ShapeDtypeStruct(w.shape, BF16),
        compiler_params=_params(("arbitrary",)),
        name="weights_to_bf16",
    )(w)


def _ada_kernel(cp_ref, cs_ref, w_ref, b_ref, mp_ref, ms_ref):
    w = w_ref[...].astype(BF16)
    b = b_ref[...]
    mp_ref[...] = _dot(_silu(cp_ref[...]).astype(BF16), w) + b
    ms_ref[...] = _dot(_silu(cs_ref[...]).astype(BF16), w) + b


def _ada(c_prompt, c_sample, w_ada, b_ada):
    nb, ns = c_prompt.shape[0], c_sample.shape[0]
    n_out = w_ada.shape[1]
    tn = D_MODEL
    return pl.pallas_call(
        _ada_kernel,
        grid=(n_out // tn,),
        in_specs=[
            pl.BlockSpec((nb, D_MODEL), lambda j: (0, 0)),
            pl.BlockSpec((ns, D_MODEL), lambda j: (0, 0)),
            pl.BlockSpec((D_MODEL, tn), lambda j: (0, j)),
            pl.BlockSpec((1, tn), lambda j: (0, j)),
        ],
        out_specs=[
            pl.BlockSpec((nb, tn), lambda j: (0, j)),
            pl.BlockSpec((ns, tn), lambda j: (0, j)),
        ],
        out_shape=[jax.ShapeDtypeStruct((nb, n_out), F32), jax.ShapeDtypeStruct((ns, n_out), F32)],
        compiler_params=_params(("arbitrary",)),
        name="ada_mod",
    )(c_prompt, c_sample, w_ada, b_ada.reshape(1, n_out))


def _head_rms_norm(t, gain, group_ones):
    sq = t * t
    hi = sq.astype(BF16)
    lo = (sq - hi.astype(F32)).astype(BF16)
    ss = _dot(hi, group_ones) + _dot(lo, group_ones)
    return t * lax.rsqrt(ss * (1.0 / HEAD_DIM) + EPS) * gain


def _inproj_kernel(*refs, host):
    if host is not None:
        pt_ref, x_ref, sh_ref, sc_ref, g1_ref, w_ref, gq_ref, gk_ref = refs[:8]
        qs_refs, kt_hbm = refs[8:8 + host.cps], refs[8 + host.cps]
        q_ref, k_ref, v_ref, kb_ref, vt_ref, u_ref, km_ref, gate_ref, page_buf, page_sem = refs[9 + host.cps:]
        slot = host.fetch(pt_ref, kt_hbm, page_buf, page_sem)
    else:
        x_ref, sh_ref, sc_ref, g1_ref, w_ref, gq_ref, gk_ref, wq32_ref, q_ref, k_ref, v_ref, u_ref = refs
    prompt = host is not None
    h = _mod_rms_norm(x_ref[0], g1_ref[...], sc_ref[0], sh_ref[0])
    hb = h.astype(BF16)

    def proj(c0, c1):
        return _dot(hb, w_ref[:, c0:c1])

    a, c = ATTN_WIDTH, CONV_CH
    row = _iota((a, a), 0)
    col = _iota((a, a), 1)
    group_ones = jnp.where((row >> 6) == (col >> 6), 1.0, 0.0).astype(BF16)

    q_pre = proj(0, a) if prompt else _dot(h, wq32_ref[...], HIGHEST)
    q = _head_rms_norm(q_pre, gq_ref[...], group_ones)
    k = _head_rms_norm(proj(a, 2 * a), gk_ref[...], group_ones)
    v = proj(2 * a, 3 * a)
    q_ref[0] = q
    k_ref[0] = k
    v_ref[0] = v
    glu_a = proj(3 * a, 3 * a + c)
    glu_b = proj(3 * a + c, 3 * a + 2 * c)
    u_ref[0] = glu_a * _sigmoid(glu_b)
    if prompt:
        for r in range(k.shape[0] // MOBA_BLOCK):
            rows = slice(r * MOBA_BLOCK, (r + 1) * MOBA_BLOCK)
            kb_ref[0, r] = k[rows].astype(BF16)
            vt_ref[0, r] = v[rows].T.astype(BF16)
            km_ref[0, r] = jnp.mean(k[rows], axis=0, keepdims=True)
        host.reduce(qs_refs, page_buf.at[slot], gate_ref)


def _inproj(x3, mod3, g_norm1, w_in_b, gq_t, gk_t, *, tm, wq32=None, gate_host=None):
    g, s, _ = x3.shape
    r = mod3.shape[1]
    nt = s // tm
    prompt = gate_host is not None
    host = _GateHost(*gate_host, grid=(g, nt)) if prompt else None

    def mod_spec(chunk):
        if r == 1:
            return pl.BlockSpec((1, 1, D_MODEL), lambda b, t, *_: (b, 0, chunk))
        return pl.BlockSpec((1, tm, D_MODEL), lambda b, t, *_: (b, t, chunk))

    tok = lambda width: pl.BlockSpec((1, tm, width), lambda b, t, *_: (b, t, 0))
    in_specs = [tok(D_MODEL), mod_spec(0), mod_spec(1),
                _resident((1, D_MODEL)), _weight_cols(N_QKVU, 0), _resident((1, ATTN_WIDTH)), _resident((1, ATTN_WIDTH))]
    operands = [x3, mod3, mod3, g_norm1.reshape(1, D_MODEL), w_in_b, gq_t, gk_t]
    out_specs = [tok(ATTN_WIDTH), tok(ATTN_WIDTH), tok(ATTN_WIDTH)]
    out_shape = [jax.ShapeDtypeStruct((g, s, ATTN_WIDTH), F32)] * 3
    if prompt:
        nblk = tm // MOBA_BLOCK
        blk_spec = lambda rows, cols: pl.BlockSpec((1, nblk, rows, cols), lambda b, t, *_: (b, t, 0, 0))
        in_specs += host.in_specs()
        operands += host.operands()
        out_specs += [blk_spec(MOBA_BLOCK, ATTN_WIDTH), blk_spec(ATTN_WIDTH, MOBA_BLOCK), tok(CONV_CH),
                      blk_spec(1, ATTN_WIDTH), host.out_spec()]
        out_shape += [jax.ShapeDtypeStruct((g, s // MOBA_BLOCK, MOBA_BLOCK, ATTN_WIDTH), BF16),
                      jax.ShapeDtypeStruct((g, s // MOBA_BLOCK, ATTN_WIDTH, MOBA_BLOCK), BF16),
                      jax.ShapeDtypeStruct((g, s, CONV_CH), F32),
                      jax.ShapeDtypeStruct((g, s // MOBA_BLOCK, 1, ATTN_WIDTH), F32),
                      host.out_shape()]
        prefetch = [host.page_table]
    else:
        in_specs += [_weight_cols(ATTN_WIDTH, 0)]
        operands += [wq32]
        out_specs += [tok(CONV_CH)]
        out_shape += [jax.ShapeDtypeStruct((g, s, CONV_CH), F32)]
        prefetch = []
    grid_spec = pltpu.PrefetchScalarGridSpec(
        num_scalar_prefetch=len(prefetch), grid=(g, nt), in_specs=in_specs, out_specs=out_specs,
        scratch_shapes=host.scratch_shapes() if prompt else [])
    return pl.pallas_call(
        functools.partial(_inproj_kernel, host=host),
        grid_spec=grid_spec,
        out_shape=out_shape,
        compiler_params=_params(("arbitrary", "arbitrary")),
        name="in_proj_prompt" if prompt else "in_proj_sample",
    )(*prefetch, *operands)


def _moba_keep_t(q_t, km, q_blk, keep_ref):
    nb = km.shape[0]
    assert nb == SUBLANES
    kmb = jnp.concatenate([km] * N_HEADS, axis=0)
    row = _iota(kmb.shape, 0)
    col = _iota(kmb.shape, 1)
    kmb = jnp.where((row >> 3) == (col >> 6), kmb, 0.0)
    gate_t = _dot(kmb, q_t, HIGHEST)
    n_idx = _iota((nb, q_t.shape[1]), 0)
    for h in range(N_HEADS):
        gate = gate_t[h * nb:(h + 1) * nb]
        beaten_by = jnp.zeros(gate.shape, jnp.int32)
        for s in range(1, nb):
            g_m = pltpu.roll(gate, s, axis=0)
            lower = n_idx >= s
            m_idx = jnp.where(lower, n_idx - s, n_idx - s + nb)
            beats = (g_m > gate) | (lower & (g_m == gate))
            beaten_by += jnp.where(beats & (m_idx < q_blk), 1, 0)
        keep = (n_idx < q_blk) & (beaten_by < MOBA_TOPK)
        keep_ref[h * nb:(h + 1) * nb, :] = jnp.where(keep, 1.0, 0.0)


def _attn_kernel(q_ref, kb_ref, vt_ref, km_ref, o_ref, qz_ref, keep_ref, bias_ref, m_ref, accl_ref, x_ref, p_ref):
    i = pl.program_id(1)
    nb = km_ref.shape[1]
    blk = MOBA_BLOCK
    q_t = q_ref[0].T
    _moba_keep_t(q_t, km_ref[0], i, keep_ref)

    key_r = _iota((blk, blk), 0)
    qry_c = _iota((blk, blk), 1)
    rel_t = (qry_c - key_r).astype(F32)
    causal = qry_c >= key_r
    qz_t = q_t * (SM_SCALE * LOG2E)
    pair_row = _iota((LANES, blk), 0)
    ones_rows = jnp.ones((BF16_ROWS, blk), BF16)
    for h in range(N_HEADS):
        hp, half = divmod(h, HEADS_PER_LANE_TILE)
        bias_ref[h] = rel_t * (-_alibi_slope(h) * LOG2E)
        qz_ref[h] = jnp.where((pair_row >> 6) == half, qz_t[hp * LANES:(hp + 1) * LANES], 0.0).astype(BF16)

    def scores(jb):
        for h in range(N_HEADS):
            hp = h // HEADS_PER_LANE_TILE
            x_ref[h] = _dot(kb_ref[0, jb, :, hp * LANES:(hp + 1) * LANES], qz_ref[h]) + bias_ref[h]

    def values(h, jb):
        return jnp.concatenate([vt_ref[0, jb, h * HEAD_DIM:(h + 1) * HEAD_DIM, :], ones_rows], axis=0)

    scores(i)
    for h in range(N_HEADS):
        x = jnp.where(causal, x_ref[h], NEG)
        m = jnp.max(x, axis=0, keepdims=True)
        p_ref[h] = jnp.exp2(x - m).astype(BF16)
        m_ref[h:h + 1, :] = m
    for h in range(N_HEADS):
        accl_ref[h] = _dot(values(h, i), p_ref[h])

    def past_block(jb, carry):
        offset = ((i - jb) * blk).astype(F32)
        scores(jb)
        alphas = []
        for h in range(N_HEADS):
            x = x_ref[h]
            c_j = offset * (_alibi_slope(h) * LOG2E)
            kept = keep_ref[pl.ds(h * nb + jb, 1), :] > 0.5
            m_old = m_ref[h:h + 1, :]
            m_new = jnp.maximum(m_old, jnp.where(kept, jnp.max(x, axis=0, keepdims=True) - c_j, NEG))
            alphas.append(jnp.exp2(m_old - m_new))
            p_ref[h] = jnp.exp2(x - jnp.where(kept, m_new + c_j, BIG)).astype(BF16)
            m_ref[h:h + 1, :] = m_new
        for h in range(N_HEADS):
            accl_ref[h] = alphas[h] * accl_ref[h] + _dot(values(h, jb), p_ref[h])
        return carry

    lax.fori_loop(0, i, past_block, 0)

    outs = [accl_ref[h, 0:HEAD_DIM, :] / accl_ref[h, HEAD_DIM:HEAD_DIM + 1, :] for h in range(N_HEADS)]
    o_ref[0] = jnp.concatenate(outs, axis=0).T.astype(BF16)


def _attn(q, kb, vt, km):
    b, s, _ = q.shape
    nq = s // MOBA_BLOCK
    return pl.pallas_call(
        _attn_kernel,
        grid=(b, nq),
        in_specs=[
            pl.BlockSpec((1, MOBA_BLOCK, ATTN_WIDTH), lambda bi, i: (bi, i, 0)),
            pl.BlockSpec((1, nq, MOBA_BLOCK, ATTN_WIDTH), lambda bi, i: (bi, 0, 0, 0)),
            pl.BlockSpec((1, nq, ATTN_WIDTH, MOBA_BLOCK), lambda bi, i: (bi, 0, 0, 0)),
            pl.BlockSpec((1, nq, ATTN_WIDTH), lambda bi, i: (bi, 0, 0)),
        ],
        out_specs=pl.BlockSpec((1, MOBA_BLOCK, ATTN_WIDTH), lambda bi, i: (bi, i, 0)),
        out_shape=jax.ShapeDtypeStruct((b, s, ATTN_WIDTH), BF16),
        scratch_shapes=[
            pltpu.VMEM((N_HEADS, LANES, MOBA_BLOCK), BF16),
            pltpu.VMEM((N_HEADS * nq, MOBA_BLOCK), F32),
            pltpu.VMEM((N_HEADS, MOBA_BLOCK, MOBA_BLOCK), F32),
            pltpu.VMEM((N_HEADS, MOBA_BLOCK), F32),
            pltpu.VMEM((N_HEADS, HEAD_DIM + BF16_ROWS, MOBA_BLOCK), F32),
            pltpu.VMEM((N_HEADS, MOBA_BLOCK, MOBA_BLOCK), F32),
            pltpu.VMEM((N_HEADS, MOBA_BLOCK, MOBA_BLOCK), BF16),
        ],
        compiler_params=_params(("arbitrary", "arbitrary")),
        name="moba_attn_prompt",
    )(q, kb, vt, km)


def _gate_weight_specs():
    first = N_QKVU // GATE_COLS
    return [_weight_cols(GATE_COLS, first + j) for j in range(2 * D_MODEL // GATE_COLS)]


def _mix_head(attn_b, x, sh1, sc1, g1, wg_refs, wap_ref):
    hb = _mod_rms_norm(x, g1, sc1, sh1).astype(BF16)
    gates = [_dot(hb, w_ref[...]) for w_ref in wg_refs]
    per_gate = len(gates) // 2
    sga = _sigmoid(jnp.concatenate(gates[:per_gate], axis=1))
    sgc = _sigmoid(jnp.concatenate(gates[per_gate:], axis=1))
    return sga * _dot(attn_b, wap_ref[...]), sgc


def _mix_tail(cv, gated_attn, sgc, x, gt1, gln, bln, wcp_ref, wo_ref):
    mu = jnp.mean(cv, axis=-1, keepdims=True)
    cen = cv - mu
    var = jnp.mean(cen * cen, axis=-1, keepdims=True)
    y = cen * lax.rsqrt(var + EPS) * gln + bln
    cvp = _dot(_silu(y).astype(BF16), wcp_ref[...])
    mix = _dot((gated_attn + sgc * cvp).astype(BF16), wo_ref[...])
    return x + gt1 * mix


def _mix_prompt_kernel(x_ref, attn_ref, u_ref, halo_ref, sh1_ref, sc1_ref, gt1_ref, g1_ref, wdw_ref, bdw_ref,
                       gln_ref, bln_ref, wg0_ref, wg1_ref, wg2_ref, wg3_ref, wap_ref, wcp_ref, wo_ref,
                       x1_ref, ext_ref, cv_ref):
    t = pl.program_id(1)
    tm = u_ref.shape[1]
    x = x_ref[0]
    gated_attn, sgc = _mix_head(attn_ref[0], x, sh1_ref[0], sc1_ref[0], g1_ref[...],
                                (wg0_ref, wg1_ref, wg2_ref, wg3_ref), wap_ref)
    ext_ref[0, 0:CONV_HALO, :] = jnp.where(t > 0, halo_ref[0], 0.0)
    ext_ref[0, CONV_HALO:, :] = u_ref[0]
    shifted_rows = tm + CONV_HALO - SUBLANES
    for r in range(1, SUBLANES):
        ext_ref[r, 0:shifted_rows, :] = ext_ref[0, r:r + shifted_rows, :]
    lead = CONV_HALO - (CONV_WIDTH - 1)
    for r0 in range(0, tm, CONV_ROWS):
        acc = jnp.zeros((CONV_ROWS // SUBLANES, SUBLANES, CONV_CH), F32)
        for w in range(CONV_WIDTH):
            r = (lead + w) % SUBLANES
            a = lead + w - r + r0
            acc = acc + ext_ref[r, a:a + CONV_ROWS, :].reshape(acc.shape) * wdw_ref[w]
        cv_ref[r0:r0 + CONV_ROWS, :] = acc.reshape(CONV_ROWS, CONV_CH) + bdw_ref[...]
    x1_ref[0] = _mix_tail(cv_ref[...], gated_attn, sgc, x, gt1_ref[0], gln_ref[...], bln_ref[...], wcp_ref, wo_ref)


def _mix_prompt(x3, attn, u, mod3, g_norm1, wdw, bdw, gln, bln, w_in_b, wap_b, wcp_b, wo_b, *, tm):
    b, s, _ = x3.shape
    nt = s // tm
    halo_per_tile = tm // CONV_HALO
    tok = lambda width: pl.BlockSpec((1, tm, width), lambda bi, t: (bi, t, 0))
    full = lambda shape: pl.BlockSpec(shape, lambda bi, t: (0,) * len(shape))
    mod_spec = lambda chunk: pl.BlockSpec((1, 1, D_MODEL), lambda bi, t: (bi, 0, chunk))
    return pl.pallas_call(
        _mix_prompt_kernel,
        grid=(b, nt),
        in_specs=[
            tok(D_MODEL), tok(ATTN_WIDTH), tok(CONV_CH),
            pl.BlockSpec((1, CONV_HALO, CONV_CH), lambda bi, t: (bi, jnp.maximum(t * halo_per_tile - 1, 0), 0)),
            mod_spec(0), mod_spec(1), mod_spec(2),
            full((1, D_MODEL)),
            full(wdw.shape), full((1, CONV_CH)), full((1, CONV_CH)), full((1, CONV_CH)),
            *_gate_weight_specs(), full(wap_b.shape), full(wcp_b.shape), full(wo_b.shape),
        ],
        out_specs=tok(D_MODEL),
        out_shape=jax.ShapeDtypeStruct((b, s, D_MODEL), F32),
        scratch_shapes=[pltpu.VMEM((SUBLANES, CONV_HALO + tm, CONV_CH), F32), pltpu.VMEM((tm, CONV_CH), F32)],
        compiler_params=_params(("arbitrary", "arbitrary")),
        name="mix_prompt",
    )(x3, attn, u, u, mod3, mod3, mod3, g_norm1.reshape(1, D_MODEL), wdw, bdw, gln, bln,
      *([w_in_b] * len(_gate_weight_specs())), wap_b, wcp_b, wo_b)


def _mix_sample_kernel(x_ref, attn_ref, u_ref, st_ref, sh1_ref, sc1_ref, gt1_ref, g1_ref, wdw_ref, bdw_ref,
                       gln_ref, bln_ref, wg0_ref, wg1_ref, wg2_ref, wg3_ref, wap_ref, wcp_ref, wo_ref,
                       x1_ref, st_out_ref):
    u = u_ref[...]
    past = CONV_WIDTH - 1
    cv = u * wdw_ref[past:past + 1, :] + bdw_ref[...]
    for w in range(past):
        cv = cv + st_ref[w] * wdw_ref[w:w + 1, :]
    for w in range(1, past):
        st_out_ref[w - 1] = st_ref[w]
    st_out_ref[past - 1] = u
    x = x_ref[...]
    gated_attn, sgc = _mix_head(attn_ref[...].astype(BF16), x, sh1_ref[...], sc1_ref[...], g1_ref[...],
                                (wg0_ref, wg1_ref, wg2_ref, wg3_ref), wap_ref)
    x1_ref[...] = _mix_tail(cv, gated_attn, sgc, x, gt1_ref[...], gln_ref[...], bln_ref[...], wcp_ref, wo_ref)


def _mix_sample(x2, attn, u, st_t, mod_s, g_norm1, wdw, bdw, gln, bln, w_in_b, wap_b, wcp_b, wo_b):
    n = x2.shape[0]
    full = lambda shape: pl.BlockSpec(shape, lambda j: (0,) * len(shape))
    mod_spec = lambda chunk: pl.BlockSpec((n, D_MODEL), lambda j: (0, chunk))
    return pl.pallas_call(
        _mix_sample_kernel,
        grid=(1,),
        in_specs=[
            full(x2.shape), full(attn.shape), full(u.shape), full(st_t.shape),
            mod_spec(0), mod_spec(1), mod_spec(2),
            full((1, D_MODEL)),
            full(wdw.shape), full((1, CONV_CH)), full((1, CONV_CH)), full((1, CONV_CH)),
            *_gate_weight_specs(), full(wap_b.shape), full(wcp_b.shape), full(wo_b.shape),
        ],
        out_specs=[full(x2.shape), full(st_t.shape)],
        out_shape=[jax.ShapeDtypeStruct(x2.shape, F32), jax.ShapeDtypeStruct(st_t.shape, F32)],
        compiler_params=_params(("arbitrary",)),
        name="mix_sample",
    )(x2, attn, u, st_t, mod_s, mod_s, mod_s, g_norm1.reshape(1, D_MODEL), wdw, bdw, gln, bln,
      *([w_in_b] * len(_gate_weight_specs())), wap_b, wcp_b, wo_b)


def _ffn_kernel(*refs, host):
    if host is not None:
        pt_ref, refs = refs[0], refs[1:]
        qs_refs, kt_hbm = refs[7:7 + host.cps], refs[7 + host.cps]
        y_ref, gate_ref, page_buf, page_sem = refs[8 + host.cps:]
        slot = host.fetch(pt_ref, kt_hbm, page_buf, page_sem)
    else:
        y_ref = refs[7]
    x_ref, sh_ref, sc_ref, gt_ref, g2_ref, wi_ref, wo_ref = refs[:7]
    x = x_ref[0]
    hb = _mod_rms_norm(x, g2_ref[...], sc_ref[0], sh_ref[0]).astype(BF16)
    acc = jnp.zeros(x.shape, F32)
    for c in range(D_FF // FFN_CHUNK):
        c0 = c * FFN_CHUNK
        f_g = _dot(hb, wi_ref[:, c0:c0 + FFN_CHUNK])
        f_u = _dot(hb, wi_ref[:, D_FF + c0:D_FF + c0 + FFN_CHUNK])
        acc = acc + _dot((_silu(f_g) * f_u).astype(BF16), wo_ref[c0:c0 + FFN_CHUNK, :])
    y_ref[0] = x + gt_ref[0] * acc
    if host is not None:
        host.reduce(qs_refs, page_buf.at[slot], gate_ref)


def _ffn(x3, mod3, g_norm2, wi_b, wo_b, *, tm, name, gate_host=None):
    g, s, _ = x3.shape
    r = mod3.shape[1]
    nt = s // tm
    host = _GateHost(*gate_host, grid=(g, nt)) if gate_host is not None else None

    def mod_spec(chunk):
        if r == 1:
            return pl.BlockSpec((1, 1, D_MODEL), lambda b, t, *_: (b, 0, chunk))
        return pl.BlockSpec((1, tm, D_MODEL), lambda b, t, *_: (b, t, chunk))

    tok = pl.BlockSpec((1, tm, D_MODEL), lambda b, t, *_: (b, t, 0))
    in_specs = [tok, mod_spec(3), mod_spec(4), mod_spec(5),
                _resident((1, D_MODEL)), _resident(wi_b.shape), _resident(wo_b.shape)]
    operands = [x3, mod3, mod3, mod3, g_norm2.reshape(1, D_MODEL), wi_b, wo_b]
    out_specs = [tok]
    out_shape = [jax.ShapeDtypeStruct(x3.shape, F32)]
    prefetch = []
    if host is not None:
        in_specs += host.in_specs()
        operands += host.operands()
        out_specs += [host.out_spec()]
        out_shape += [host.out_shape()]
        prefetch = [host.page_table]
    grid_spec = pltpu.PrefetchScalarGridSpec(
        num_scalar_prefetch=len(prefetch), grid=(g, nt), in_specs=in_specs, out_specs=out_specs,
        scratch_shapes=host.scratch_shapes() if host is not None else [])
    return pl.pallas_call(
        functools.partial(_ffn_kernel, host=host),
        grid_spec=grid_spec,
        out_shape=out_shape,
        compiler_params=_params(("arbitrary", "arbitrary")),
        name=name,
    )(*prefetch, *operands)


HOST_CHUNK_PAGES = 16
HOST_BLOCKS = HOST_CHUNK_PAGES // PAGES_PER_BLOCK
IN_PROJ_HOST_CHUNKS = 1
FFN_HOST_CHUNKS = 3
PAGE_DMA_PRIORITY = 1


def _chunk_gate(qs_ref, pages_ref, first_page, chunk_in_seq):
    qcol = jnp.broadcast_to(qs_ref[0], (LANES, ATTN_WIDTH)).T
    sub = _iota((N_HEADS, LANES), 0)
    lane = _iota((N_HEADS, LANES), 1)
    gate = jnp.zeros((N_HEADS, LANES), F32)
    for h in range(N_HEADS):
        q_h = qcol[h * HEAD_DIM:(h + 1) * HEAD_DIM]
        col = jnp.zeros((SUBLANES, LANES), F32)
        for blk in range(HOST_BLOCKS):
            kt = pages_ref[first_page + blk * PAGES_PER_BLOCK, h]
            for pg in range(1, PAGES_PER_BLOCK):
                kt = kt + pages_ref[first_page + blk * PAGES_PER_BLOCK + pg, h]
            part = jnp.sum((kt * q_h).reshape(HEAD_DIM // SUBLANES, SUBLANES, PAGE_SIZE), axis=0)
            col = jnp.where(lane == chunk_in_seq * HOST_BLOCKS + blk, jnp.sum(part, axis=1, keepdims=True), col)
        row = jnp.sum(col, axis=0, keepdims=True) * (1.0 / MOBA_BLOCK)
        gate = jnp.where(sub == h, row, gate)
    return gate


class _GateHost:
    def __init__(self, page_table, q_s3, cache_kt, first_chunk, chunks_per_step, grid):
        self.chunks_per_seq = page_table.shape[1] // HOST_CHUNK_PAGES
        self.cps = chunks_per_step
        self.steps = grid[0] * grid[1]
        self.nt = grid[1]
        assert first_chunk + self.steps * self.cps <= page_table.shape[0] * self.chunks_per_seq
        self.page_table, self.q_s3, self.cache_kt, self.first_chunk = page_table, q_s3, cache_kt, first_chunk

    def _chunk(self, step, j):
        return self.first_chunk + step * self.cps + j

    def in_specs(self):
        def q_spec(j):
            return pl.BlockSpec((1, 1, ATTN_WIDTH),
                                lambda b, t, pt: (self._chunk(b * self.nt + t, j) // self.chunks_per_seq, 0, 0))
        return [q_spec(j) for j in range(self.cps)] + [pl.BlockSpec(memory_space=pl.ANY)]

    def operands(self):
        return [self.q_s3] * self.cps + [self.cache_kt]

    def scratch_shapes(self):
        pages = self.cps * HOST_CHUNK_PAGES
        return [pltpu.VMEM((2, pages) + self.cache_kt.shape[1:], F32), pltpu.SemaphoreType.DMA((2,))]

    def _copies(self, pt_ref, kt_hbm, page_buf, page_sem, step, slot, for_wait):
        copies = []
        for j in range(self.cps):
            chunk = self._chunk(step, j)
            seq = chunk // self.chunks_per_seq
            first = (chunk % self.chunks_per_seq) * HOST_CHUNK_PAGES
            for r in range(HOST_CHUNK_PAGES):
                src = kt_hbm.at[0 if for_wait else pt_ref[seq, first + r]]
                dst = page_buf.at[slot, j * HOST_CHUNK_PAGES + r]
                copies.append(pltpu.make_async_copy(src, dst, page_sem.at[slot]))
        return copies

    def fetch(self, pt_ref, kt_hbm, page_buf, page_sem):
        step = pl.program_id(0) * self.nt + pl.program_id(1)
        slot = step % 2

        @pl.when(step == 0)
        def _():
            for c in self._copies(pt_ref, kt_hbm, page_buf, page_sem, step, slot, False):
                c.start(priority=PAGE_DMA_PRIORITY)

        @pl.when(step + 1 < self.steps)
        def _():
            for c in self._copies(pt_ref, kt_hbm, page_buf, page_sem, step + 1, 1 - slot, False):
                c.start(priority=PAGE_DMA_PRIORITY)

        for c in self._copies(pt_ref, kt_hbm, page_buf, page_sem, step, slot, True):
            c.wait()
        return slot

    def reduce(self, qs_refs, pages_ref, gate_ref):
        step = pl.program_id(0) * self.nt + pl.program_id(1)
        for j in range(self.cps):
            chunk_in_seq = self._chunk(step, j) % self.chunks_per_seq
            gate_ref[0, j] = _chunk_gate(qs_refs[j], pages_ref, j * HOST_CHUNK_PAGES, chunk_in_seq)

    def out_spec(self):
        return pl.BlockSpec((1, self.cps, N_HEADS, LANES), lambda b, t, pt: (b * self.nt + t, 0, 0, 0))

    def out_shape(self):
        return jax.ShapeDtypeStruct((self.steps, self.cps, N_HEADS, LANES), F32)


def _topk_kernel(gp_ref, idx_ref):
    chunks = gp_ref.shape[1]
    g = gp_ref[:, 0]
    for c in range(1, chunks):
        g = g + gp_ref[:, c]
    lane = _iota(g.shape, 2)
    g = jnp.where(lane < chunks * HOST_BLOCKS, g, -jnp.inf)
    out = jnp.zeros(g.shape, jnp.int32)
    for r in range(MOBA_TOPK):
        best = jnp.max(g, axis=2, keepdims=True)
        pick = jnp.min(jnp.where(g == best, lane, LANES), axis=2, keepdims=True)
        out = jnp.where(lane == r, pick, out)
        g = jnp.where(lane == pick, -jnp.inf, g)
    idx_ref[...] = out


def _topk(gate_parts):
    nseq = gate_parts.shape[0]
    out_shape = (nseq, N_HEADS, LANES)
    return pl.pallas_call(
        _topk_kernel,
        grid=(1,),
        in_specs=[pl.BlockSpec(gate_parts.shape, lambda j: (0, 0, 0, 0))],
        out_specs=pl.BlockSpec(out_shape, lambda j: (0, 0, 0)),
        out_shape=jax.ShapeDtypeStruct(out_shape, jnp.int32),
        compiler_params=_params(("arbitrary",)),
        name="moba_topk_sample",
    )(gate_parts)


SEL_PAGES = MOBA_TOPK * PAGES_PER_BLOCK
SATTN_HEADS_PER_STEP = 4


def _sattn_kernel(pt_ref, idx_ref, q_ref, kn_ref, vn_ref, *refs, past_len):
    del pt_ref
    n_pg = SATTN_HEADS_PER_STEP * SEL_PAGES
    k_refs = refs[:n_pg]
    v_refs = refs[n_pg:2 * n_pg]
    o_ref = refs[2 * n_pg]
    b = pl.program_id(0)
    hg = pl.program_id(1)
    lane = _iota((1, PAGE_SIZE), 1)
    eye = _iota((HEAD_DIM, HEAD_DIM), 0) == _iota((HEAD_DIM, HEAD_DIM), 1)
    for hh in range(SATTN_HEADS_PER_STEP):
        h = hg * SATTN_HEADS_PER_STEP + hh
        q_row = q_ref[0, hh] * SM_SCALE
        q = jnp.sum(jnp.where(eye, q_row, 0.0), axis=1, keepdims=True)
        k_cat = jnp.concatenate([k_refs[hh * SEL_PAGES + j][0, 0] for j in range(SEL_PAGES)], axis=1)
        v_cat = jnp.concatenate([v_refs[hh * SEL_PAGES + j][0, 0] for j in range(SEL_PAGES)], axis=1)
        pos = []
        for r in range(MOBA_TOPK):
            blk = idx_ref[(b * N_HEADS + h) * MOBA_TOPK + r]
            pos += [blk * MOBA_BLOCK + pg * PAGE_SIZE + lane for pg in range(PAGES_PER_BLOCK)]
        dist = (past_len - jnp.concatenate(pos, axis=1)).astype(F32)
        slope = lax.bitcast_convert_type(jnp.full(dist.shape, 126, jnp.int32) - h << 23, F32)
        s = jnp.sum(k_cat * q, axis=0, keepdims=True) - slope * dist
        s_own = jnp.sum(q_row * kn_ref[0, hh], axis=1, keepdims=True)
        m = jnp.maximum(jnp.max(s, axis=1, keepdims=True), s_own)
        p = jnp.exp(s - m)
        p_own = jnp.exp(s_own - m)
        l = jnp.sum(p, axis=1, keepdims=True) + p_own
        acc = jnp.sum(v_cat * p, axis=1, keepdims=True)
        acc_row = jnp.sum(jnp.where(eye, acc, 0.0), axis=0, keepdims=True)
        o_ref[0, hh] = (acc_row + p_own * vn_ref[0, hh]) / l


def _sattn(page_table, idx_flat, q4, kn4, vn4, cache_kt, cache_vt):
    nseq, n_pages = page_table.shape
    past_len = n_pages * PAGE_SIZE
    hps = SATTN_HEADS_PER_STEP

    def page_spec(hh, r, pg):
        def index_map(b, hg, pt, idx):
            h = hg * hps + hh
            return (pt[b, idx[(b * N_HEADS + h) * MOBA_TOPK + r] * PAGES_PER_BLOCK + pg], h, 0, 0)
        return pl.BlockSpec((1, 1, HEAD_DIM, PAGE_SIZE), index_map)

    tok = pl.BlockSpec((1, hps, 1, HEAD_DIM), lambda b, hg, pt, idx: (b, hg, 0, 0))
    pages = [page_spec(hh, r, pg) for hh in range(hps) for r in range(MOBA_TOPK) for pg in range(PAGES_PER_BLOCK)]
    grid_spec = pltpu.PrefetchScalarGridSpec(
        num_scalar_prefetch=2,
        grid=(nseq, N_HEADS // hps),
        in_specs=[tok, tok, tok] + pages + pages,
        out_specs=tok,
    )
    return pl.pallas_call(
        functools.partial(_sattn_kernel, past_len=past_len),
        grid_spec=grid_spec,
        out_shape=jax.ShapeDtypeStruct((nseq, N_HEADS, 1, HEAD_DIM), F32),
        compiler_params=_params(("arbitrary", "arbitrary")),
        name="moba_attn_sample",
    )(page_table, idx_flat, q4, kn4, vn4, *([cache_kt] * len(pages)), *([cache_vt] * len(pages)))


def kernel(x_prompt, x_sample, cache_k, cache_v, state_conv, page_table, c_prompt, c_sample, w_ada, b_ada, g_norm1, w_in, g_q, g_k, w_attn_proj, w_dwconv, b_dwconv, g_conv_ln, b_conv_ln, w_conv_proj, w_out, g_norm2, w_ffn_in, w_ffn_out):
    nb, seq, _ = x_prompt.shape
    ns = x_sample.shape[0]

    w_in_b = _to_bf16(w_in)
    wap_b = w_attn_proj.astype(BF16)
    wcp_b = w_conv_proj.astype(BF16)
    wo_b = w_out.astype(BF16)
    wfi_b = _to_bf16(w_ffn_in)
    wfo_b = _to_bf16(w_ffn_out)
    gq_t = jnp.tile(g_q, N_HEADS).reshape(1, ATTN_WIDTH)
    gk_t = jnp.tile(g_k, N_HEADS).reshape(1, ATTN_WIDTH)
    wdw = w_dwconv.reshape(CONV_WIDTH, CONV_CH)
    bdw = b_dwconv.reshape(1, CONV_CH)
    gln = g_conv_ln.reshape(1, CONV_CH)
    bln = b_conv_ln.reshape(1, CONV_CH)

    mod_p, mod_s = _ada(c_prompt, c_sample, w_ada, b_ada)
    mod_p3 = mod_p.reshape(nb, 1, 6 * D_MODEL)
    mod_s3 = mod_s.reshape(1, ns, 6 * D_MODEL)

    xs3 = x_sample.reshape(1, ns, D_MODEL)
    qs, ks, vs, us = _inproj(xs3, mod_s3, g_norm1, w_in_b, gq_t, gk_t, tm=ns, wq32=w_in)
    qs3 = qs.reshape(ns, 1, ATTN_WIDTH)
    cache_kt = jnp.transpose(cache_k, (0, 2, 3, 1))
    cache_vt = jnp.transpose(cache_v, (0, 2, 3, 1))

    host_a = (page_table, qs3, cache_kt, 0, IN_PROJ_HOST_CHUNKS)
    q, k, v, kb, vt, u, km, gates_a = _inproj(x_prompt, mod_p3, g_norm1, w_in_b, gq_t, gk_t, tm=256, gate_host=host_a)
    attn = _attn(q, kb, vt, km.reshape(nb, seq // MOBA_BLOCK, ATTN_WIDTH))
    wdw8 = jnp.broadcast_to(wdw[:, None, :], (CONV_WIDTH, SUBLANES, CONV_CH))
    x1 = _mix_prompt(x_prompt, attn, u, mod_p3, g_norm1, wdw8, bdw, gln, bln, w_in_b, wap_b, wcp_b, wo_b, tm=256)
    host_b = (page_table, qs3, cache_kt, gates_a.shape[0] * gates_a.shape[1], FFN_HOST_CHUNKS)
    y_prompt, gates_b = _ffn(x1, mod_p3, g_norm2, wfi_b, wfo_b, tm=256, name="ffn_prompt", gate_host=host_b)
    chunks_per_seq = page_table.shape[1] // HOST_CHUNK_PAGES
    gates = jnp.concatenate([g.reshape(-1, chunks_per_seq, N_HEADS, LANES) for g in (gates_a, gates_b)], axis=0)
    assert gates.shape[0] == ns
    k_prompt = k.reshape(nb, seq, N_HEADS, HEAD_DIM)
    v_prompt = v.reshape(nb, seq, N_HEADS, HEAD_DIM)
    conv_prompt = u[:, seq - (CONV_WIDTH - 1):, :]

    idx = _topk(gates)
    idx_flat = idx[:, :, :MOBA_TOPK].reshape(-1)
    head4 = lambda t: t.reshape(ns, N_HEADS, 1, HEAD_DIM)
    attn_s = _sattn(page_table, idx_flat, head4(qs), head4(ks), head4(vs), cache_kt, cache_vt)
    x1_s, st_new = _mix_sample(
        xs3[0], attn_s.reshape(ns, ATTN_WIDTH), us[0], jnp.transpose(state_conv, (1, 0, 2)),
        mod_s, g_norm1, wdw, bdw, gln, bln, w_in_b, wap_b, wcp_b, wo_b)
    (y_sample,) = _ffn(x1_s.reshape(1, ns, D_MODEL), mod_s3, g_norm2, wfi_b, wfo_b, tm=ns, name="ffn_sample")

    return (y_prompt, y_sample.reshape(ns, 1, D_MODEL), k_prompt, v_prompt, conv_prompt,
            ks.reshape(ns, 1, N_HEADS, HEAD_DIM), vs.reshape(ns, 1, N_HEADS, HEAD_DIM),
            jnp.transpose(st_new, (1, 0, 2)))
```
